```python
import jax, jax.numpy as jnp
from jax import lax
import numpy as np

D_MODEL = 2048
BATCH = 16
SEQ = 2048
DEPTH = 2
DEC_BATCH = 1
DEC_SEQ = 8192
PAST_LEN = 128

N_EVEN = (DEPTH + 1) // 2
N_ODD = DEPTH // 2
CONV_DIM = D_MODEL // 2
CONV_WIDTH = 3
MLA_HEADS = 8
NOPE_DIM = 128
ROPE_DIM = 64
V_DIM = 128
QK_DIM = NOPE_DIM + ROPE_DIM
Q_RANK = D_MODEL // 4
KV_RANK = D_MODEL // 8
IN0_DIM = 3 * CONV_DIM + Q_RANK + KV_RANK + ROPE_DIM
MIX0_OUT = CONV_DIM + MLA_HEADS * V_DIM
POOL_WINDOWS = (2, 4, 8, 16)
POOL_GROUP = D_MODEL // len(POOL_WINDOWS)
D_FF = 4 * D_MODEL
ROPE_THETA = 10000.0
Q_BLOCK = 128
EPS = 1e-6

kernel_name = "hybrid_conv_mla_pool_encoder"


def rmsnorm(x, g):
    xf = x.astype(jnp.float32)
    y = xf * lax.rsqrt(jnp.mean(xf * xf, axis=-1, keepdims=True) + EPS)
    return (y * g.astype(jnp.float32)).astype(x.dtype)


def rope_tables(seq):
    inv = 1.0 / (ROPE_THETA ** (jnp.arange(0, ROPE_DIM, 2, dtype=jnp.float32) / ROPE_DIM))
    ang = jnp.arange(seq, dtype=jnp.float32)[:, None] * inv[None, :]
    return jnp.cos(ang), jnp.sin(ang)


def apply_rope(x, cos, sin):
    x1, x2 = jnp.split(x.astype(jnp.float32), 2, axis=-1)
    c = cos[None, :, None, :]
    s = sin[None, :, None, :]
    return jnp.concatenate([x1 * c - x2 * s, x1 * s + x2 * c], axis=-1).astype(x.dtype)


def short_conv_mixer(xb, xc, xi, conv_w):
    s = xi.shape[1]
    u = xc * xi
    up = jnp.pad(u, ((0, 0), (1, 1), (0, 0)))
    v = conv_w[0] * up[:, :s] + conv_w[1] * up[:, 1:s + 1] + conv_w[2] * up[:, 2:]
    return xb * v


def mla_mixer(cq, ckv, kr, q_a_norm, w_qb, kv_a_norm, w_kvb, q_norm, k_norm):
    bn, s, _ = cq.shape
    q = (rmsnorm(cq, q_a_norm) @ w_qb).reshape(bn, s, MLA_HEADS, QK_DIM)
    kv = (rmsnorm(ckv, kv_a_norm) @ w_kvb).reshape(bn, s, MLA_HEADS, NOPE_DIM + V_DIM)
    k_nope, v = kv[..., :NOPE_DIM], kv[..., NOPE_DIM:]
    k = jnp.concatenate([k_nope, jnp.broadcast_to(kr[:, :, None, :], (bn, s, MLA_HEADS, ROPE_DIM))], axis=-1)
    q = rmsnorm(q, q_norm)
    k = rmsnorm(k, k_norm)
    cos, sin = rope_tables(s)
    q = jnp.concatenate([q[..., :NOPE_DIM], apply_rope(q[..., NOPE_DIM:], cos, sin)], axis=-1)
    k = jnp.concatenate([k[..., :NOPE_DIM], apply_rope(k[..., NOPE_DIM:], cos, sin)], axis=-1)
    scale = QK_DIM ** -0.5
    nblk = s // Q_BLOCK
    qb = q.reshape(bn, nblk, Q_BLOCK, MLA_HEADS, QK_DIM).transpose(1, 0, 2, 3, 4)

    def block(qi):
        sc = jnp.einsum('bqhd,bkhd->bhqk', qi, k, preferred_element_type=jnp.float32) * scale
        p = jax.nn.softmax(sc, axis=-1)
        return jnp.einsum('bhqk,bkhd->bqhd', p.astype(v.dtype), v)

    o = lax.map(block, qb)
    return o.transpose(1, 0, 2, 3, 4).reshape(bn, s, MLA_HEADS * V_DIM)


def pool_mixer(h, w_pool, pool_scale):
    bn, s, d = h.shape
    hf = h.astype(jnp.float32)
    cs = jnp.concatenate([jnp.zeros((bn, 1, d), jnp.float32), jnp.cumsum(hf, axis=1)], axis=1)
    t = jnp.arange(s)
    outs = []
    for g, w in enumerate(POOL_WINDOWS):
        lo_c, hi_c = g * POOL_GROUP, (g + 1) * POOL_GROUP
        lo = jnp.clip(t - w // 2, 0, s)
        hi = jnp.clip(t + w // 2, 0, s)
        csg = cs[:, :, lo_c:hi_c]
        cnt = (hi - lo).astype(jnp.float32)[None, :, None]
        mean = (csg[:, hi] - csg[:, lo]) / cnt
        p = (mean - hf[:, :, lo_c:hi_c]).astype(h.dtype)
        outs.append(p @ w_pool[g])
    return jnp.concatenate(outs, axis=-1) * pool_scale


def trunk(x, norm_mix0, w_in0, conv_w, q_a_norm, w_qb, kv_a_norm, w_kvb, q_norm, k_norm, w_o0,
          norm_mix1, w_pool, pool_scale, norm_mlp, w_up, w_down):
    splits = [CONV_DIM, 2 * CONV_DIM, 3 * CONV_DIM, 3 * CONV_DIM + Q_RANK, 3 * CONV_DIM + Q_RANK + KV_RANK]
    for i in range(DEPTH):
        j = i // 2
        if i % 2 == 0:
            h = rmsnorm(x, norm_mix0[j])
            u = h @ w_in0[j]
            xb, xc, xi, cq, ckv, kr = jnp.split(u, splits, axis=-1)
            a_out = short_conv_mixer(xb, xc, xi, conv_w[j])
            b_out = mla_mixer(cq, ckv, kr, q_a_norm[j], w_qb[j], kv_a_norm[j], w_kvb[j], q_norm[j], k_norm[j])
            x = x + jnp.concatenate([a_out, b_out], axis=-1) @ w_o0[j]
        else:
            h = rmsnorm(x, norm_mix1[j])
            x = x + pool_mixer(h, w_pool[j], pool_scale[j])
        h = rmsnorm(x, norm_mlp[i])
        x = x + jnp.square(jax.nn.relu(h @ w_up[i])) @ w_down[i]
    return x


def setup_inputs(seed: int = 0) -> dict:
    key = jax.random.key(seed)
    ks = jax.random.split(key, 20)
    f32 = jnp.float32

    def nrm(k, shape, fan_in):
        return jax.random.normal(k, shape, f32) * (fan_in ** -0.5)

    def gain(k, shape):
        return 1.0 + 0.02 * jax.random.normal(k, shape, f32)

    return {
        "x_prompt": jax.random.normal(ks[0], (BATCH, SEQ, D_MODEL), f32),
        "x_sample": jax.random.normal(ks[1], (DEC_BATCH, DEC_SEQ, D_MODEL), f32),
        "norm_mix0": gain(ks[2], (N_EVEN, D_MODEL)),
        "w_in0": nrm(ks[3], (N_EVEN, D_MODEL, IN0_DIM), D_MODEL),
        "conv_w": nrm(ks[4], (N_EVEN, CONV_WIDTH, CONV_DIM), CONV_WIDTH),
        "q_a_norm": gain(ks[5], (N_EVEN, Q_RANK)),
        "w_qb": nrm(ks[6], (N_EVEN, Q_RANK, MLA_HEADS * QK_DIM), Q_RANK),
        "kv_a_norm": gain(ks[7], (N_EVEN, KV_RANK)),
        "w_kvb": nrm(ks[8], (N_EVEN, KV_RANK, MLA_HEADS * (NOPE_DIM + V_DIM)), KV_RANK),
        "q_norm": gain(ks[9], (N_EVEN, QK_DIM)),
        "k_norm": gain(ks[10], (N_EVEN, QK_DIM)),
        "w_o0": nrm(ks[11], (N_EVEN, MIX0_OUT, D_MODEL), MIX0_OUT),
        "norm_mix1": gain(ks[12], (N_ODD, D_MODEL)),
        "w_pool": nrm(ks[13], (N_ODD, len(POOL_WINDOWS), POOL_GROUP, POOL_GROUP), POOL_GROUP),
        "pool_scale": gain(ks[14], (N_ODD, D_MODEL)),
        "norm_mlp": gain(ks[15], (DEPTH, D_MODEL)),
        "w_up": nrm(ks[16], (DEPTH, D_MODEL, D_FF), D_MODEL),
        "w_down": nrm(ks[17], (DEPTH, D_FF, D_MODEL), D_FF),
    }


def reference(x_prompt, x_sample, norm_mix0, w_in0, conv_w, q_a_norm, w_qb, kv_a_norm, w_kvb, q_norm, k_norm,
              w_o0, norm_mix1, w_pool, pool_scale, norm_mlp, w_up, w_down):
    y_prompt = trunk(x_prompt, norm_mix0, w_in0, conv_w, q_a_norm, w_qb, kv_a_norm, w_kvb, q_norm, k_norm, w_o0,
                     norm_mix1, w_pool, pool_scale, norm_mlp, w_up, w_down)
    y_sample = trunk(x_sample, norm_mix0, w_in0, conv_w, q_a_norm, w_qb, kv_a_norm, w_kvb, q_norm, k_norm, w_o0,
                     norm_mix1, w_pool, pool_scale, norm_mlp, w_up, w_down)
    return (y_prompt, y_sample)
```

```python
import functools

import jax
import jax.numpy as jnp
from jax import lax
from jax.experimental import pallas as pl
from jax.experimental.pallas import tpu as pltpu

F32 = jnp.float32
BF16 = jnp.bfloat16

D_MODEL = 2048
CONV_DIM = D_MODEL // 2
MLA_HEADS = 8
NOPE_DIM = 128
ROPE_DIM = 64
V_DIM = 128
QK_DIM = NOPE_DIM + ROPE_DIM
Q_RANK = D_MODEL // 4
KV_RANK = D_MODEL // 8
IN0_DIM = 3 * CONV_DIM + Q_RANK + KV_RANK + ROPE_DIM
IN0_PAD = 4096
POOL_WINDOWS = (2, 4, 8, 16)
POOL_GROUP = D_MODEL // len(POOL_WINDOWS)
POOL_HALO = 8
D_FF = 4 * D_MODEL
ROPE_THETA = 10000.0
EPS = 1e-6

LANES = 128
BF16_SUBLANES = 16
VMEM_LIMIT = 56 * 1024 * 1024


def _params(*sem):
    return pltpu.CompilerParams(dimension_semantics=sem, vmem_limit_bytes=VMEM_LIMIT)


def _rms(x, g):
    ms = jnp.mean(x * x, axis=-1, keepdims=True)
    return x * lax.rsqrt(ms + EPS) * g


def _inproj_kernel(x_ref, g_ref, w_ref, o_ref, h_ref):
    @pl.when(pl.program_id(1) == 0)
    def _():
        h_ref[...] = _rms(x_ref[...], g_ref[...]).astype(BF16)

    o_ref[...] = jnp.dot(h_ref[...], w_ref[...], preferred_element_type=F32).astype(o_ref.dtype)


def _inproj(x, g, w, tm, tn):
    t, d = x.shape
    n = w.shape[1]
    return pl.pallas_call(
        _inproj_kernel,
        out_shape=jax.ShapeDtypeStruct((t, n), BF16),
        grid=(t // tm, n // tn),
        in_specs=[
            pl.BlockSpec((tm, d), lambda i, j: (i, 0)),
            pl.BlockSpec((1, d), lambda i, j: (0, 0)),
            pl.BlockSpec((d, tn), lambda i, j: (0, j)),
        ],
        out_specs=pl.BlockSpec((tm, tn), lambda i, j: (i, j)),
        scratch_shapes=[pltpu.VMEM((tm, d), BF16)],
        compiler_params=_params("parallel", "arbitrary"),
        name="inproj",
    )(x, g, w)


def _rope(r, cos, sin_signed, first_half):
    rot = jnp.where(first_half, pltpu.roll(r, LANES - ROPE_DIM // 2, 1), pltpu.roll(r, ROPE_DIM // 2, 1))
    return r * cos + rot * sin_signed


def _prep_kernel(cq_ref, ckv_ref, kr_ref, cos_ref, sin_ref, gqa_ref, gkva_ref, wqb_ref, wkvb_ref,
                 gqn_ref, gqr_ref, gkn_ref, gkr_ref, qn_ref, qr_ref, k_ref, v_ref):
    cqn = _rms(cq_ref[...].astype(F32), gqa_ref[...]).astype(BF16)
    q = jnp.dot(cqn, wqb_ref[...], preferred_element_type=F32)
    ckvn = _rms(ckv_ref[...].astype(F32), gkva_ref[...]).astype(BF16)
    kv = jnp.dot(ckvn, wkvb_ref[...], preferred_element_type=F32)
    kr2 = kr_ref[...].astype(F32)

    lane = lax.broadcasted_iota(jnp.int32, (1, LANES), 1)
    lo = lane < ROPE_DIM
    first_half = (lane % ROPE_DIM) < (ROPE_DIM // 2)
    cos = cos_ref[...]
    sin = sin_ref[...]
    gqn = gqn_ref[...]
    gqr = gqr_ref[...]
    gkn = gkn_ref[...]
    gkr = gkr_ref[...]
    inv_d = 1.0 / QK_DIM
    nope_all = MLA_HEADS * NOPE_DIM

    for j in range(MLA_HEADS // 2):
        rp = q[:, nope_all + j * LANES: nope_all + (j + 1) * LANES]
        rp2 = rp * rp
        s_pair = (jnp.sum(jnp.where(lo, rp2, 0.0), axis=-1, keepdims=True),
                  jnp.sum(jnp.where(lo, 0.0, rp2), axis=-1, keepdims=True))
        r_pair = []
        for e in range(2):
            h = 2 * j + e
            nh = q[:, h * NOPE_DIM:(h + 1) * NOPE_DIM]
            ss = jnp.sum(nh * nh, axis=-1, keepdims=True) + s_pair[e]
            r_h = lax.rsqrt(ss * inv_d + EPS)
            qn_ref[:, h * NOPE_DIM:(h + 1) * NOPE_DIM] = (nh * r_h * gqn).astype(BF16)
            r_pair.append(r_h)
        rscale = jnp.where(lo, r_pair[0], r_pair[1])
        qr_ref[:, j * LANES:(j + 1) * LANES] = _rope(rp * rscale * gqr, cos, sin, first_half).astype(BF16)

    kr2sq = kr2 * kr2
    s_kr = jnp.sum(jnp.where(lo, kr2sq, 0.0), axis=-1, keepdims=True)
    rope_k = _rope(kr2 * gkr, cos, sin, first_half)
    hw = NOPE_DIM + V_DIM
    for h in range(MLA_HEADS):
        kn = kv[:, h * hw: h * hw + NOPE_DIM]
        vh = kv[:, h * hw + NOPE_DIM:(h + 1) * hw]
        ss = jnp.sum(kn * kn, axis=-1, keepdims=True) + s_kr
        r_h = lax.rsqrt(ss * inv_d + EPS)
        k_ref[:, h * hw: h * hw + NOPE_DIM] = (kn * r_h * gkn).astype(BF16)
        keep = lo if h % 2 == 0 else jnp.logical_not(lo)
        k_ref[:, h * hw + NOPE_DIM:(h + 1) * hw] = jnp.where(keep, rope_k * r_h, 0.0).astype(BF16)
        v_ref[:, h * V_DIM:(h + 1) * V_DIM] = vh.astype(BF16)


def _prep(u, cos, sin, gqa, gkva, wqb, wkvb, gqn, gqr, gkn, gkr, seq, tm):
    t = u.shape[0]
    nseq = seq // tm
    full = lambda a: pl.BlockSpec(a.shape, lambda i: (0,) * a.ndim)
    cq_blk = 3 * CONV_DIM // Q_RANK
    ckv_blk = (3 * CONV_DIM + Q_RANK) // KV_RANK
    kr_blk = (3 * CONV_DIM + Q_RANK + KV_RANK) // LANES
    return pl.pallas_call(
        _prep_kernel,
        out_shape=(
            jax.ShapeDtypeStruct((t, MLA_HEADS * NOPE_DIM), BF16),
            jax.ShapeDtypeStruct((t, MLA_HEADS * ROPE_DIM), BF16),
            jax.ShapeDtypeStruct((t, MLA_HEADS * (NOPE_DIM + V_DIM)), BF16),
            jax.ShapeDtypeStruct((t, MLA_HEADS * V_DIM), BF16),
        ),
        grid=(t // tm,),
        in_specs=[
            pl.BlockSpec((tm, Q_RANK), lambda i: (i, cq_blk)),
            pl.BlockSpec((tm, KV_RANK), lambda i: (i, ckv_blk)),
            pl.BlockSpec((tm, LANES), lambda i: (i, kr_blk)),
            pl.BlockSpec((tm, LANES), lambda i: (i % nseq, 0)),
            pl.BlockSpec((tm, LANES), lambda i: (i % nseq, 0)),
            full(gqa), full(gkva), full(wqb), full(wkvb), full(gqn), full(gqr), full(gkn), full(gkr),
        ],
        out_specs=(
            pl.BlockSpec((tm, MLA_HEADS * NOPE_DIM), lambda i: (i, 0)),
            pl.BlockSpec((tm, MLA_HEADS * ROPE_DIM), lambda i: (i, 0)),
            pl.BlockSpec((tm, MLA_HEADS * (NOPE_DIM + V_DIM)), lambda i: (i, 0)),
            pl.BlockSpec((tm, MLA_HEADS * V_DIM), lambda i: (i, 0)),
        ),
        compiler_params=_params("parallel"),
        name="mla_prep",
    )(u, u, u, cos, sin, gqa, gkva, wqb, wkvb, gqn, gqr, gkn, gkr)


def _flash_kernel(qn_ref, qr_ref, k_ref, v_ref, o_ref, *, tk, nk):
    q = jnp.concatenate([qn_ref[...], qr_ref[...]], axis=1)
    tq = q.shape[0]

    def body(c, carry):
        m, l, acc = carry
        start = pl.multiple_of(c * tk, tk)
        ks = k_ref[pl.ds(start, tk), :]
        vs = v_ref[pl.ds(start, tk), :]
        s = lax.dot_general(q, ks, (((1,), (1,)), ((), ())), preferred_element_type=F32)
        m_new = jnp.maximum(m, jnp.max(s, axis=-1, keepdims=True))
        alpha = jnp.exp(m - m_new)
        p = jnp.exp(s - m_new)
        l = alpha * l + jnp.sum(p, axis=-1, keepdims=True)
        acc = alpha * acc + jnp.dot(p.astype(BF16), vs, preferred_element_type=F32)
        return m_new, l, acc

    init = (jnp.full((tq, 1), -jnp.inf, F32), jnp.zeros((tq, 1), F32), jnp.zeros((tq, V_DIM), F32))
    _, l, acc = lax.fori_loop(0, nk, body, init)
    o_ref[...] = (acc / l).astype(o_ref.dtype)


def _flash(qn, qr, k, v, batch, seq, tq, tk):
    t = qn.shape[0]
    nq = seq // tq
    return pl.pallas_call(
        functools.partial(_flash_kernel, tk=tk, nk=seq // tk),
        out_shape=jax.ShapeDtypeStruct((t, MLA_HEADS * V_DIM), BF16),
        grid=(batch, MLA_HEADS, nq),
        in_specs=[
            pl.BlockSpec((tq, NOPE_DIM), lambda b, h, i: (b * nq + i, h)),
            pl.BlockSpec((tq, LANES), lambda b, h, i: (b * nq + i, h // 2)),
            pl.BlockSpec((seq, NOPE_DIM + V_DIM), lambda b, h, i: (b, h)),
            pl.BlockSpec((seq, V_DIM), lambda b, h, i: (b, h)),
        ],
        out_specs=pl.BlockSpec((tq, V_DIM), lambda b, h, i: (b * nq + i, h)),
        compiler_params=_params("parallel", "parallel", "arbitrary"),
        name="flash",
    )(qn, qr, k, v)


def _outproj_kernel(x_ref, xb_ref, xc_ref, xi_ref, xcp_ref, xip_ref, xcn_ref, xin_ref, cw_ref, att_ref,
                    wo_ref, o_ref, *, tm, seq):
    pos0 = lax.rem(pl.program_id(0) * tm, seq)
    u = xc_ref[...].astype(F32) * xi_ref[...].astype(F32)
    halo_p = xcp_ref[...].astype(F32) * xip_ref[...].astype(F32)
    halo_n = xcn_ref[...].astype(F32) * xin_ref[...].astype(F32)
    u_before = jnp.where(pos0 != 0, halo_p[BF16_SUBLANES - 1:BF16_SUBLANES, :], 0.0)
    u_after = jnp.where(pos0 + tm != seq, halo_n[0:1, :], 0.0)
    row = lax.broadcasted_iota(jnp.int32, (tm, 1), 0)
    u_m1 = jnp.where(row == 0, u_before, pltpu.roll(u, 1, 0))
    u_p1 = jnp.where(row == tm - 1, u_after, pltpu.roll(u, tm - 1, 0))
    cw = cw_ref[...]
    conv = cw[0:1, :] * u_m1 + cw[1:2, :] * u + cw[2:3, :] * u_p1
    a = (xb_ref[...].astype(F32) * conv).astype(BF16)
    y = jnp.dot(a, wo_ref[0:CONV_DIM, :], preferred_element_type=F32)
    y = y + jnp.dot(att_ref[...], wo_ref[CONV_DIM:, :], preferred_element_type=F32)
    o_ref[...] = x_ref[...] + y


def _outproj(x, u, cw, att, wo, seq, tm):
    t, d = x.shape
    rb = tm // BF16_SUBLANES
    last = t // BF16_SUBLANES - 1
    prev_map = lambda c: (lambda i: (jnp.maximum(i * rb - 1, 0), c))
    next_map = lambda c: (lambda i: (jnp.minimum((i + 1) * rb, last), c))
    return pl.pallas_call(
        functools.partial(_outproj_kernel, tm=tm, seq=seq),
        out_shape=jax.ShapeDtypeStruct((t, d), F32),
        grid=(t // tm,),
        in_specs=[
            pl.BlockSpec((tm, d), lambda i: (i, 0)),
            pl.BlockSpec((tm, CONV_DIM), lambda i: (i, 0)),
            pl.BlockSpec((tm, CONV_DIM), lambda i: (i, 1)),
            pl.BlockSpec((tm, CONV_DIM), lambda i: (i, 2)),
            pl.BlockSpec((BF16_SUBLANES, CONV_DIM), prev_map(1)),
            pl.BlockSpec((BF16_SUBLANES, CONV_DIM), prev_map(2)),
            pl.BlockSpec((BF16_SUBLANES, CONV_DIM), next_map(1)),
            pl.BlockSpec((BF16_SUBLANES, CONV_DIM), next_map(2)),
            pl.BlockSpec(cw.shape, lambda i: (0, 0)),
            pl.BlockSpec((tm, MLA_HEADS * V_DIM), lambda i: (i, 0)),
            pl.BlockSpec(wo.shape, lambda i: (0, 0)),
        ],
        out_specs=pl.BlockSpec((tm, d), lambda i: (i, 0)),
        compiler_params=_params("parallel"),
        name="outproj",
    )(x, u, u, u, u, u, u, u, cw, att, wo)


def _pool_kernel(x_ref, xp_ref, xn_ref, g_ref, wp_ref, ps_ref, o_ref, *, tm, seq):
    pos0 = lax.rem(pl.program_id(0) * tm, seq)
    x = x_ref[...]
    g = g_ref[...]
    xp = jnp.where(pos0 != 0, xp_ref[...], 0.0)
    xn = jnp.where(pos0 + tm != seq, xn_ref[...], 0.0)
    hext = _rms(jnp.concatenate([xp, x, xn], axis=0), g)
    n = tm + 2 * POOL_HALO
    pos = pos0 + lax.broadcasted_iota(jnp.int32, (tm, 1), 0)
    ps = ps_ref[...]
    for gi, w in enumerate(POOL_WINDOWS):
        c0, c1 = gi * POOL_GROUP, (gi + 1) * POOL_GROUP
        hg = hext[:, c0:c1]
        half = w // 2
        run = hg
        step = 1
        while step < half:
            run = run + pltpu.roll(run, n - step, 0)
            step *= 2
        win = run + pltpu.roll(run, half, 0)
        win = win[POOL_HALO:POOL_HALO + tm, :]
        cnt = jnp.minimum(pos + half, seq) - jnp.maximum(pos - half, 0)
        mean = win / cnt.astype(F32)
        p = (mean - hg[POOL_HALO:POOL_HALO + tm, :]).astype(BF16)
        y = jnp.dot(p, wp_ref[gi], preferred_element_type=F32)
        o_ref[:, c0:c1] = x[:, c0:c1] + y * ps[:, c0:c1]


def _pool(x, g, wp, ps, seq, tm):
    t, d = x.shape
    rb = tm // POOL_HALO
    last = t // POOL_HALO - 1
    return pl.pallas_call(
        functools.partial(_pool_kernel, tm=tm, seq=seq),
        out_shape=jax.ShapeDtypeStruct((t, d), F32),
        grid=(t // tm,),
        in_specs=[
            pl.BlockSpec((tm, d), lambda i: (i, 0)),
            pl.BlockSpec((POOL_HALO, d), lambda i: (jnp.maximum(i * rb - 1, 0), 0)),
            pl.BlockSpec((POOL_HALO, d), lambda i: (jnp.minimum((i + 1) * rb, last), 0)),
            pl.BlockSpec((1, d), lambda i: (0, 0)),
            pl.BlockSpec(wp.shape, lambda i: (0, 0, 0)),
            pl.BlockSpec((1, d), lambda i: (0, 0)),
        ],
        out_specs=pl.BlockSpec((tm, d), lambda i: (i, 0)),
        compiler_params=_params("parallel"),
        name="pool",
    )(x, x, x, g, wp, ps)


def _mlp_kernel(x_ref, g_ref, wu_ref, wd_ref, o_ref, h_ref):
    @pl.when(pl.program_id(1) == 0)
    def _():
        x = x_ref[...]
        h_ref[...] = _rms(x, g_ref[...]).astype(BF16)
        o_ref[...] = x

    a = jnp.dot(h_ref[...], wu_ref[...], preferred_element_type=F32)
    a = jnp.square(jnp.maximum(a, 0.0)).astype(BF16)
    o_ref[...] += jnp.dot(a, wd_ref[...], preferred_element_type=F32)


def _mlp(x, g, wu, wd, tm, tf):
    t, d = x.shape
    f = wu.shape[1]
    return pl.pallas_call(
        _mlp_kernel,
        out_shape=jax.ShapeDtypeStruct((t, d), F32),
        grid=(t // tm, f // tf),
        in_specs=[
            pl.BlockSpec((tm, d), lambda i, j: (i, 0)),
            pl.BlockSpec((1, d), lambda i, j: (0, 0)),
            pl.BlockSpec((d, tf), lambda i, j: (0, j)),
            pl.BlockSpec((tf, d), lambda i, j: (j, 0)),
        ],
        out_specs=pl.BlockSpec((tm, d), lambda i, j: (i, 0)),
        scratch_shapes=[pltpu.VMEM((tm, d), BF16)],
        compiler_params=_params("parallel", "arbitrary"),
        name="mlp",
    )(x, g, wu, wd)


def _rope_tables(seq):
    inv = 1.0 / (ROPE_THETA ** (jnp.arange(0, ROPE_DIM, 2, dtype=F32) / ROPE_DIM))
    ang = jnp.arange(seq, dtype=F32)[:, None] * inv[None, :]
    cos, sin = jnp.cos(ang), jnp.sin(ang)
    reps = LANES // (ROPE_DIM // 2)
    sign = jnp.tile(jnp.concatenate([-jnp.ones((ROPE_DIM // 2,), F32), jnp.ones((ROPE_DIM // 2,), F32)]),
                    LANES // ROPE_DIM)
    return jnp.tile(cos, (1, reps)), jnp.tile(sin, (1, reps)) * sign[None, :]


def _prepare_even(norm_mix0, w_in0, conv_w, q_a_norm, w_qb, kv_a_norm, w_kvb, q_norm, k_norm, w_o0):
    kr_cols = w_in0[:, IN0_DIM - ROPE_DIM:]
    pad = jnp.zeros((D_MODEL, IN0_PAD - IN0_DIM - ROPE_DIM), w_in0.dtype)
    w_in = jnp.concatenate([w_in0, kr_cols, pad], axis=1).astype(BF16)
    wq3 = w_qb.reshape(Q_RANK, MLA_HEADS, QK_DIM)
    wqb = jnp.concatenate([wq3[:, :, :NOPE_DIM].reshape(Q_RANK, MLA_HEADS * NOPE_DIM),
                           wq3[:, :, NOPE_DIM:].reshape(Q_RANK, MLA_HEADS * ROPE_DIM)], axis=1).astype(BF16)
    scale = QK_DIM ** -0.5
    return dict(
        g0=norm_mix0.reshape(1, D_MODEL), w_in=w_in, cw=conv_w,
        gqa=q_a_norm.reshape(1, Q_RANK), gkva=kv_a_norm.reshape(1, KV_RANK),
        wqb=wqb, wkvb=w_kvb.astype(BF16),
        gqn=(q_norm[:NOPE_DIM] * scale).reshape(1, NOPE_DIM),
        gqr=(jnp.tile(q_norm[NOPE_DIM:], LANES // ROPE_DIM) * scale).reshape(1, LANES),
        gkn=k_norm[:NOPE_DIM].reshape(1, NOPE_DIM),
        gkr=jnp.tile(k_norm[NOPE_DIM:], LANES // ROPE_DIM).reshape(1, LANES),
        wo=w_o0.astype(BF16),
    )


def _tile(n, pref):
    return pref if n % pref == 0 else n


def _trunk(x3, even, odd, mlp, cos, sin):
    batch, seq, d = x3.shape
    t = batch * seq
    x = x3.reshape(t, d)
    tm_mlp = _tile(seq, 1024)
    tm = _tile(seq, 512)
    for i, (gm, wu, wd) in enumerate(mlp):
        j = i // 2
        if i % 2 == 0:
            p = even[j]
            u = _inproj(x, p["g0"], p["w_in"], tm_mlp, 1024)
            qn, qr, k, v = _prep(u, cos, sin, p["gqa"], p["gkva"], p["wqb"], p["wkvb"],
                                 p["gqn"], p["gqr"], p["gkn"], p["gkr"], seq, tm)
            att = _flash(qn, qr, k, v, batch, seq, tm, _tile(seq, 512))
            x = _outproj(x, u, p["cw"], att, p["wo"], seq, tm)
        else:
            g1, wp, ps = odd[j]
            x = _pool(x, g1, wp, ps, seq, tm)
        x = _mlp(x, gm, wu, wd, tm_mlp, 512)
    return x.reshape(batch, seq, d)


def kernel(x_prompt, x_sample, norm_mix0, w_in0, conv_w, q_a_norm, w_qb, kv_a_norm, w_kvb, q_norm, k_norm, w_o0,
           norm_mix1, w_pool, pool_scale, norm_mlp, w_up, w_down):
    depth = norm_mlp.shape[0]
    even = [_prepare_even(norm_mix0[j], w_in0[j], conv_w[j], q_a_norm[j], w_qb[j], kv_a_norm[j], w_kvb[j],
                          q_norm[j], k_norm[j], w_o0[j]) for j in range(norm_mix0.shape[0])]
    odd = [(norm_mix1[j].reshape(1, D_MODEL), w_pool[j].astype(BF16), pool_scale[j].reshape(1, D_MODEL))
           for j in range(norm_mix1.shape[0])]
    mlp = [(norm_mlp[i].reshape(1, D_MODEL), w_up[i].astype(BF16), w_down[i].astype(BF16)) for i in range(depth)]
    outs = []
    for x3 in (x_prompt, x_sample):
        cos, sin = _rope_tables(x3.shape[1])
        outs.append(_trunk(x3, even, odd, mlp, cos, sin))
    return tuple(outs)
```

```python
import functools

import jax
import jax.numpy as jnp
from jax import lax
from jax.experimental import pallas as pl
from jax.experimental.pallas import tpu as pltpu

F32 = jnp.float32
BF16 = jnp.bfloat16

D_MODEL = 2048
CONV_DIM = D_MODEL // 2
MLA_HEADS = 8
NOPE_DIM = 128
ROPE_DIM = 64
V_DIM = 128
QK_DIM = NOPE_DIM + ROPE_DIM
Q_RANK = D_MODEL // 4
KV_RANK = D_MODEL // 8
IN0_DIM = 3 * CONV_DIM + Q_RANK + KV_RANK + ROPE_DIM
IN0_PAD = 4096
POOL_WINDOWS = (2, 4, 8, 16)
POOL_GROUP = D_MODEL // len(POOL_WINDOWS)
POOL_HALO = 8
D_FF = 4 * D_MODEL
ROPE_THETA = 10000.0
EPS = 1e-6

LANES = 128
BF16_SUBLANES = 16
VMEM_LIMIT = 56 * 1024 * 1024


def _params(*sem):
    return pltpu.CompilerParams(dimension_semantics=sem, vmem_limit_bytes=VMEM_LIMIT)


def _rms(x, g):
    ms = jnp.mean(x * x, axis=-1, keepdims=True)
    return x * lax.rsqrt(ms + EPS) * g


def _inproj_kernel(x_ref, g_ref, w_ref, o_ref, h_ref):
    @pl.when(pl.program_id(1) == 0)
    def _():
        h_ref[...] = _rms(x_ref[...], g_ref[...]).astype(BF16)

    o_ref[...] = jnp.dot(h_ref[...], w_ref[...], preferred_element_type=F32).astype(o_ref.dtype)


def _inproj(x, g, w, tm, tn):
    t, d = x.shape
    n = w.shape[1]
    return pl.pallas_call(
        _inproj_kernel,
        out_shape=jax.ShapeDtypeStruct((t, n), BF16),
        grid=(t // tm, n // tn),
        in_specs=[
            pl.BlockSpec((tm, d), lambda i, j: (i, 0)),
            pl.BlockSpec((1, d), lambda i, j: (0, 0)),
            pl.BlockSpec((d, tn), lambda i, j: (0, j)),
        ],
        out_specs=pl.BlockSpec((tm, tn), lambda i, j: (i, j)),
        scratch_shapes=[pltpu.VMEM((tm, d), BF16)],
        compiler_params=_params("parallel", "arbitrary"),
        name="inproj",
    )(x, g, w)


def _rope(r, cos, sin_signed, first_half):
    rot = jnp.where(first_half, pltpu.roll(r, LANES - ROPE_DIM // 2, 1), pltpu.roll(r, ROPE_DIM // 2, 1))
    return r * cos + rot * sin_signed


def _prep_kernel(cq_ref, ckv_ref, kr_ref, cos_ref, sin_ref, gqa_ref, gkva_ref, wqb_ref, wkvb_ref,
                 gqn_ref, gqr_ref, gkn_ref, gkr_ref, qn_ref, qr_ref, k_ref, v_ref):
    cqn = _rms(cq_ref[...].astype(F32), gqa_ref[...]).astype(BF16)
    q = jnp.dot(cqn, wqb_ref[...], preferred_element_type=F32)
    ckvn = _rms(ckv_ref[...].astype(F32), gkva_ref[...]).astype(BF16)
    kv = jnp.dot(ckvn, wkvb_ref[...], preferred_element_type=F32)
    kr2 = kr_ref[...].astype(F32)

    lane = lax.broadcasted_iota(jnp.int32, (1, LANES), 1)
    lo = lane < ROPE_DIM
    first_half = (lane % ROPE_DIM) < (ROPE_DIM // 2)
    cos = cos_ref[...]
    sin = sin_ref[...]
    gqn = gqn_ref[...]
    gqr = gqr_ref[...]
    gkn = gkn_ref[...]
    gkr = gkr_ref[...]
    inv_d = 1.0 / QK_DIM
    nope_all = MLA_HEADS * NOPE_DIM

    for j in range(MLA_HEADS // 2):
        rp = q[:, nope_all + j * LANES: nope_all + (j + 1) * LANES]
        rp2 = rp * rp
        s_pair = (jnp.sum(jnp.where(lo, rp2, 0.0), axis=-1, keepdims=True),
                  jnp.sum(jnp.where(lo, 0.0, rp2), axis=-1, keepdims=True))
        r_pair = []
        for e in range(2):
            h = 2 * j + e
            nh = q[:, h * NOPE_DIM:(h + 1) * NOPE_DIM]
            ss = jnp.sum(nh * nh, axis=-1, keepdims=True) + s_pair[e]
            r_h = lax.rsqrt(ss * inv_d + EPS)
            qn_ref[:, h * NOPE_DIM:(h + 1) * NOPE_DIM] = (nh * r_h * gqn).astype(BF16)
            r_pair.append(r_h)
        rscale = jnp.where(lo, r_pair[0], r_pair[1])
        qr_ref[:, j * LANES:(j + 1) * LANES] = _rope(rp * rscale * gqr, cos, sin, first_half).astype(BF16)

    kr2sq = kr2 * kr2
    s_kr = jnp.sum(jnp.where(lo, kr2sq, 0.0), axis=-1, keepdims=True)
    rope_k = _rope(kr2 * gkr, cos, sin, first_half)
    hw = NOPE_DIM + V_DIM
    for h in range(MLA_HEADS):
        kn = kv[:, h * hw: h * hw + NOPE_DIM]
        vh = kv[:, h * hw + NOPE_DIM:(h + 1) * hw]
        ss = jnp.sum(kn * kn, axis=-1, keepdims=True) + s_kr
        r_h = lax.rsqrt(ss * inv_d + EPS)
        k_ref[:, h * hw: h * hw + NOPE_DIM] = (kn * r_h * gkn).astype(BF16)
        keep = lo if h % 2 == 0 else jnp.logical_not(lo)
        k_ref[:, h * hw + NOPE_DIM:(h + 1) * hw] = jnp.where(keep, rope_k * r_h, 0.0).astype(BF16)
        v_ref[:, h * V_DIM:(h + 1) * V_DIM] = vh.astype(BF16)


def _prep(u, cos, sin, gqa, gkva, wqb, wkvb, gqn, gqr, gkn, gkr, seq, tm):
    t = u.shape[0]
    nseq = seq // tm
    full = lambda a: pl.BlockSpec(a.shape, lambda i: (0,) * a.ndim)
    cq_blk = 3 * CONV_DIM // Q_RANK
    ckv_blk = (3 * CONV_DIM + Q_RANK) // KV_RANK
    kr_blk = (3 * CONV_DIM + Q_RANK + KV_RANK) // LANES
    return pl.pallas_call(
        _prep_kernel,
        out_shape=(
            jax.ShapeDtypeStruct((t, MLA_HEADS * NOPE_DIM), BF16),
            jax.ShapeDtypeStruct((t, MLA_HEADS * ROPE_DIM), BF16),
            jax.ShapeDtypeStruct((t, MLA_HEADS * (NOPE_DIM + V_DIM)), BF16),
            jax.ShapeDtypeStruct((t, MLA_HEADS * V_DIM), BF16),
        ),
        grid=(t // tm,),
        in_specs=[
            pl.BlockSpec((tm, Q_RANK), lambda i: (i, cq_blk)),
            pl.BlockSpec((tm, KV_RANK), lambda i: (i, ckv_blk)),
            pl.BlockSpec((tm, LANES), lambda i: (i, kr_blk)),
            pl.BlockSpec((tm, LANES), lambda i: (i % nseq, 0)),
            pl.BlockSpec((tm, LANES), lambda i: (i % nseq, 0)),
            full(gqa), full(gkva), full(wqb), full(wkvb), full(gqn), full(gqr), full(gkn), full(gkr),
        ],
        out_specs=(
            pl.BlockSpec((tm, MLA_HEADS * NOPE_DIM), lambda i: (i, 0)),
            pl.BlockSpec((tm, MLA_HEADS * ROPE_DIM), lambda i: (i, 0)),
            pl.BlockSpec((tm, MLA_HEADS * (NOPE_DIM + V_DIM)), lambda i: (i, 0)),
            pl.BlockSpec((tm, MLA_HEADS * V_DIM), lambda i: (i, 0)),
        ),
        compiler_params=_params("parallel"),
        name="mla_prep",
    )(u, u, u, cos, sin, gqa, gkva, wqb, wkvb, gqn, gqr, gkn, gkr)


def _flash_kernel(qn_ref, qr_ref, k_ref, v_ref, o_ref, *, sub, ck):
    tq = qn_ref.shape[0]
    seq = k_ref.shape[0]
    ones = jnp.ones((ck, LANES), BF16)
    vs = [jnp.concatenate([v_ref[c:c + ck, :], ones], axis=1) for c in range(0, seq, ck)]
    for r in range(0, tq, sub):
        q = jnp.concatenate([qn_ref[r:r + sub, :], qr_ref[r:r + sub, :]], axis=1)
        m = acc = None
        for ci, c in enumerate(range(0, seq, ck)):
            s = lax.dot_general(q, k_ref[c:c + ck, :], (((1,), (1,)), ((), ())), preferred_element_type=F32)
            mc = jnp.max(s, axis=-1, keepdims=True)
            m_new = mc if m is None else jnp.maximum(m, mc)
            p = jnp.exp(s - m_new)
            pv = jnp.dot(p.astype(BF16), vs[ci], preferred_element_type=F32)
            acc = pv if m is None else jnp.exp(m - m_new) * acc + pv
            m = m_new
        o_ref[r:r + sub, :] = (acc[:, :V_DIM] / acc[:, V_DIM:V_DIM + 1]).astype(o_ref.dtype)


def _flash(qn, qr, k, v, batch, seq, tq, sub, ck):
    t = qn.shape[0]
    nq = seq // tq
    return pl.pallas_call(
        functools.partial(_flash_kernel, sub=sub, ck=ck),
        out_shape=jax.ShapeDtypeStruct((t, MLA_HEADS * V_DIM), BF16),
        grid=(batch, MLA_HEADS, nq),
        in_specs=[
            pl.BlockSpec((tq, NOPE_DIM), lambda b, h, i: (b * nq + i, h)),
            pl.BlockSpec((tq, LANES), lambda b, h, i: (b * nq + i, h // 2)),
            pl.BlockSpec((seq, NOPE_DIM + V_DIM), lambda b, h, i: (b, h)),
            pl.BlockSpec((seq, V_DIM), lambda b, h, i: (b, h)),
        ],
        out_specs=pl.BlockSpec((tq, V_DIM), lambda b, h, i: (b * nq + i, h)),
        compiler_params=_params("parallel", "parallel", "arbitrary"),
        name="flash",
    )(qn, qr, k, v)


def _outproj_kernel(x_ref, xb_ref, xc_ref, xi_ref, xcp_ref, xip_ref, xcn_ref, xin_ref, cw_ref, att_ref,
                    wo_ref, o_ref, *, tm, seq):
    pos0 = lax.rem(pl.program_id(0) * tm, seq)
    u = xc_ref[...].astype(F32) * xi_ref[...].astype(F32)
    halo_p = xcp_ref[...].astype(F32) * xip_ref[...].astype(F32)
    halo_n = xcn_ref[...].astype(F32) * xin_ref[...].astype(F32)
    u_before = jnp.where(pos0 != 0, halo_p[BF16_SUBLANES - 1:BF16_SUBLANES, :], 0.0)
    u_after = jnp.where(pos0 + tm != seq, halo_n[0:1, :], 0.0)
    row = lax.broadcasted_iota(jnp.int32, (tm, 1), 0)
    u_m1 = jnp.where(row == 0, u_before, pltpu.roll(u, 1, 0))
    u_p1 = jnp.where(row == tm - 1, u_after, pltpu.roll(u, tm - 1, 0))
    cw = cw_ref[...]
    conv = cw[0:1, :] * u_m1 + cw[1:2, :] * u + cw[2:3, :] * u_p1
    a = (xb_ref[...].astype(F32) * conv).astype(BF16)
    y = jnp.dot(a, wo_ref[0:CONV_DIM, :], preferred_element_type=F32)
    y = y + jnp.dot(att_ref[...], wo_ref[CONV_DIM:, :], preferred_element_type=F32)
    o_ref[...] = x_ref[...] + y


def _outproj(x, u, cw, att, wo, seq, tm):
    t, d = x.shape
    rb = tm // BF16_SUBLANES
    last = t // BF16_SUBLANES - 1
    prev_map = lambda c: (lambda i: (jnp.maximum(i * rb - 1, 0), c))
    next_map = lambda c: (lambda i: (jnp.minimum((i + 1) * rb, last), c))
    return pl.pallas_call(
        functools.partial(_outproj_kernel, tm=tm, seq=seq),
        out_shape=jax.ShapeDtypeStruct((t, d), F32),
        grid=(t // tm,),
        in_specs=[
            pl.BlockSpec((tm, d), lambda i: (i, 0)),
            pl.BlockSpec((tm, CONV_DIM), lambda i: (i, 0)),
            pl.BlockSpec((tm, CONV_DIM), lambda i: (i, 1)),
            pl.BlockSpec((tm, CONV_DIM), lambda i: (i, 2)),
            pl.BlockSpec((BF16_SUBLANES, CONV_DIM), prev_map(1)),
            pl.BlockSpec((BF16_SUBLANES, CONV_DIM), prev_map(2)),
            pl.BlockSpec((BF16_SUBLANES, CONV_DIM), next_map(1)),
            pl.BlockSpec((BF16_SUBLANES, CONV_DIM), next_map(2)),
            pl.BlockSpec(cw.shape, lambda i: (0, 0)),
            pl.BlockSpec((tm, MLA_HEADS * V_DIM), lambda i: (i, 0)),
            pl.BlockSpec(wo.shape, lambda i: (0, 0)),
        ],
        out_specs=pl.BlockSpec((tm, d), lambda i: (i, 0)),
        compiler_params=_params("parallel"),
        name="outproj",
    )(x, u, u, u, u, u, u, u, cw, att, wo)


def _pool_kernel(x_ref, xp_ref, xn_ref, g_ref, wp_ref, ps_ref, o_ref, *, tm, seq):
    pos0 = lax.rem(pl.program_id(0) * tm, seq)
    x = x_ref[...]
    g = g_ref[...]
    xp = jnp.where(pos0 != 0, xp_ref[...], 0.0)
    xn = jnp.where(pos0 + tm != seq, xn_ref[...], 0.0)
    hext = _rms(jnp.concatenate([xp, x, xn], axis=0), g)
    n = tm + 2 * POOL_HALO
    pos = pos0 + lax.broadcasted_iota(jnp.int32, (tm, 1), 0)
    ps = ps_ref[...]
    for gi, w in enumerate(POOL_WINDOWS):
        c0, c1 = gi * POOL_GROUP, (gi + 1) * POOL_GROUP
        hg = hext[:, c0:c1]
        half = w // 2
        run = hg
        step = 1
        while step < half:
            run = run + pltpu.roll(run, n - step, 0)
            step *= 2
        win = run + pltpu.roll(run, half, 0)
        win = win[POOL_HALO:POOL_HALO + tm, :]
        cnt = jnp.minimum(pos + half, seq) - jnp.maximum(pos - half, 0)
        mean = win * (1.0 / cnt.astype(F32))
        p = (mean - hg[POOL_HALO:POOL_HALO + tm, :]).astype(BF16)
        y = jnp.dot(p, wp_ref[gi], preferred_element_type=F32)
        o_ref[:, c0:c1] = x[:, c0:c1] + y * ps[:, c0:c1]


def _pool(x, g, wp, ps, seq, tm):
    t, d = x.shape
    rb = tm // POOL_HALO
    last = t // POOL_HALO - 1
    return pl.pallas_call(
        functools.partial(_pool_kernel, tm=tm, seq=seq),
        out_shape=jax.ShapeDtypeStruct((t, d), F32),
        grid=(t // tm,),
        in_specs=[
            pl.BlockSpec((tm, d), lambda i: (i, 0)),
            pl.BlockSpec((POOL_HALO, d), lambda i: (jnp.maximum(i * rb - 1, 0), 0)),
            pl.BlockSpec((POOL_HALO, d), lambda i: (jnp.minimum((i + 1) * rb, last), 0)),
            pl.BlockSpec((1, d), lambda i: (0, 0)),
            pl.BlockSpec(wp.shape, lambda i: (0, 0, 0)),
            pl.BlockSpec((1, d), lambda i: (0, 0)),
        ],
        out_specs=pl.BlockSpec((tm, d), lambda i: (i, 0)),
        compiler_params=_params("parallel"),
        name="pool",
    )(x, x, x, g, wp, ps)


def _mlp_kernel(x_ref, g_ref, wu_ref, wd_ref, o_ref, h_ref):
    @pl.when(pl.program_id(1) == 0)
    def _():
        x = x_ref[...]
        h_ref[...] = _rms(x, g_ref[...]).astype(BF16)
        o_ref[...] = x

    a = jnp.dot(h_ref[...], wu_ref[...], preferred_element_type=F32)
    a = jnp.square(jnp.maximum(a, 0.0)).astype(BF16)
    o_ref[...] += jnp.dot(a, wd_ref[...], preferred_element_type=F32)


def _mlp(x, g, wu, wd, tm, tf):
    t, d = x.shape
    f = wu.shape[1]
    return pl.pallas_call(
        _mlp_kernel,
        out_shape=jax.ShapeDtypeStruct((t, d), F32),
        grid=(t // tm, f // tf),
        in_specs=[
            pl.BlockSpec((tm, d), lambda i, j: (i, 0)),
            pl.BlockSpec((1, d), lambda i, j: (0, 0)),
            pl.BlockSpec((d, tf), lambda i, j: (0, j)),
            pl.BlockSpec((tf, d), lambda i, j: (j, 0)),
        ],
        out_specs=pl.BlockSpec((tm, d), lambda i, j: (i, 0)),
        scratch_shapes=[pltpu.VMEM((tm, d), BF16)],
        compiler_params=_params("parallel", "arbitrary"),
        name="mlp",
    )(x, g, wu, wd)


def _rope_tables(seq):
    inv = 1.0 / (ROPE_THETA ** (jnp.arange(0, ROPE_DIM, 2, dtype=F32) / ROPE_DIM))
    ang = jnp.arange(seq, dtype=F32)[:, None] * inv[None, :]
    cos, sin = jnp.cos(ang), jnp.sin(ang)
    reps = LANES // (ROPE_DIM // 2)
    sign = jnp.tile(jnp.concatenate([-jnp.ones((ROPE_DIM // 2,), F32), jnp.ones((ROPE_DIM // 2,), F32)]),
                    LANES // ROPE_DIM)
    return jnp.tile(cos, (1, reps)), jnp.tile(sin, (1, reps)) * sign[None, :]


def _prepare_even(norm_mix0, w_in0, conv_w, q_a_norm, w_qb, kv_a_norm, w_kvb, q_norm, k_norm, w_o0):
    kr_cols = w_in0[:, IN0_DIM - ROPE_DIM:]
    pad = jnp.zeros((D_MODEL, IN0_PAD - IN0_DIM - ROPE_DIM), w_in0.dtype)
    w_in = jnp.concatenate([w_in0, kr_cols, pad], axis=1).astype(BF16)
    wq3 = w_qb.reshape(Q_RANK, MLA_HEADS, QK_DIM)
    wqb = jnp.concatenate([wq3[:, :, :NOPE_DIM].reshape(Q_RANK, MLA_HEADS * NOPE_DIM),
                           wq3[:, :, NOPE_DIM:].reshape(Q_RANK, MLA_HEADS * ROPE_DIM)], axis=1).astype(BF16)
    scale = QK_DIM ** -0.5
    return dict(
        g0=norm_mix0.reshape(1, D_MODEL), w_in=w_in, cw=conv_w,
        gqa=q_a_norm.reshape(1, Q_RANK), gkva=kv_a_norm.reshape(1, KV_RANK),
        wqb=wqb, wkvb=w_kvb.astype(BF16),
        gqn=(q_norm[:NOPE_DIM] * scale).reshape(1, NOPE_DIM),
        gqr=(jnp.tile(q_norm[NOPE_DIM:], LANES // ROPE_DIM) * scale).reshape(1, LANES),
        gkn=k_norm[:NOPE_DIM].reshape(1, NOPE_DIM),
        gkr=jnp.tile(k_norm[NOPE_DIM:], LANES // ROPE_DIM).reshape(1, LANES),
        wo=w_o0.astype(BF16),
    )


def _tile(n, pref):
    return pref if n % pref == 0 else n


def _trunk(x3, even, odd, mlp, cos, sin):
    batch, seq, d = x3.shape
    t = batch * seq
    x = x3.reshape(t, d)
    tm_mlp = _tile(seq, 1024)
    tm = _tile(seq, 512)
    for i, (gm, wu, wd) in enumerate(mlp):
        j = i // 2
        if i % 2 == 0:
            p = even[j]
            u = _inproj(x, p["g0"], p["w_in"], tm_mlp, 1024)
            qn, qr, k, v = _prep(u, cos, sin, p["gqa"], p["gkva"], p["wqb"], p["wkvb"],
                                 p["gqn"], p["gqr"], p["gkn"], p["gkr"], seq, tm)
            att = _flash(qn, qr, k, v, batch, seq, seq if seq <= 2048 else 1024, 256, _tile(seq, 2048))
            x = _outproj(x, u, p["cw"], att, p["wo"], seq, tm)
        else:
            g1, wp, ps = odd[j]
            x = _pool(x, g1, wp, ps, seq, tm)
        x = _mlp(x, gm, wu, wd, tm_mlp, 512)
    return x.reshape(batch, seq, d)


def kernel(x_prompt, x_sample, norm_mix0, w_in0, conv_w, q_a_norm, w_qb, kv_a_norm, w_kvb, q_norm, k_norm, w_o0,
           norm_mix1, w_pool, pool_scale, norm_mlp, w_up, w_down):
    depth = norm_mlp.shape[0]
    even = [_prepare_even(norm_mix0[j], w_in0[j], conv_w[j], q_a_norm[j], w_qb[j], kv_a_norm[j], w_kvb[j],
                          q_norm[j], k_norm[j], w_o0[j]) for j in range(norm_mix0.shape[0])]
    odd = [(norm_mix1[j].reshape(1, D_MODEL), w_pool[j].astype(BF16), pool_scale[j].reshape(1, D_MODEL))
           for j in range(norm_mix1.shape[0])]
    mlp = [(norm_mlp[i].reshape(1, D_MODEL), w_up[i].astype(BF16), w_down[i].astype(BF16)) for i in range(depth)]
    outs = []
    for x3 in (x_prompt, x_sample):
        cos, sin = _rope_tables(x3.shape[1])
        outs.append(_trunk(x3, even, odd, mlp, cos, sin))
    return tuple(outs)
```

```python
import functools

import jax
import jax.numpy as jnp
from jax import lax
from jax.experimental import pallas as pl
from jax.experimental.pallas import tpu as pltpu

F32 = jnp.float32
BF16 = jnp.bfloat16

D_MODEL = 2048
CONV_DIM = D_MODEL // 2
MLA_HEADS = 8
NOPE_DIM = 128
ROPE_DIM = 64
V_DIM = 128
QK_DIM = NOPE_DIM + ROPE_DIM
Q_RANK = D_MODEL // 4
KV_RANK = D_MODEL // 8
IN0_DIM = 3 * CONV_DIM + Q_RANK + KV_RANK + ROPE_DIM
IN0_PAD = 4096
POOL_WINDOWS = (2, 4, 8, 16)
POOL_GROUP = D_MODEL // len(POOL_WINDOWS)
POOL_HALO = 8
D_FF = 4 * D_MODEL
ROPE_THETA = 10000.0
EPS = 1e-6

LANES = 128
BF16_SUBLANES = 16
VMEM_LIMIT = 56 * 1024 * 1024


def _params(*sem):
    return pltpu.CompilerParams(dimension_semantics=sem, vmem_limit_bytes=VMEM_LIMIT)


def _rms(x, g):
    ms = jnp.mean(x * x, axis=-1, keepdims=True)
    return x * lax.rsqrt(ms + EPS) * g


def _inproj_kernel(x_ref, g_ref, w_ref, o_ref, h_ref):
    @pl.when(pl.program_id(1) == 0)
    def _():
        h_ref[...] = _rms(x_ref[...], g_ref[...]).astype(BF16)

    o_ref[...] = jnp.dot(h_ref[...], w_ref[...], preferred_element_type=F32).astype(o_ref.dtype)


def _inproj(x, g, w, tm, tn):
    t, d = x.shape
    n = w.shape[1]
    return pl.pallas_call(
        _inproj_kernel,
        out_shape=jax.ShapeDtypeStruct((t, n), BF16),
        grid=(t // tm, n // tn),
        in_specs=[
            pl.BlockSpec((tm, d), lambda i, j: (i, 0)),
            pl.BlockSpec((1, d), lambda i, j: (0, 0)),
            pl.BlockSpec((d, tn), lambda i, j: (0, j)),
        ],
        out_specs=pl.BlockSpec((tm, tn), lambda i, j: (i, j)),
        scratch_shapes=[pltpu.VMEM((tm, d), BF16)],
        compiler_params=_params("parallel", "arbitrary"),
        name="inproj",
    )(x, g, w)


def _rope(r, cos, sin_signed):
    return r * cos + pltpu.roll(r, LANES // 2, 1) * sin_signed


def _prep_kernel(cq_ref, ckv_ref, kr_ref, cos_ref, sin_ref, gqa_ref, gkva_ref, wqb_ref, wkvb_ref,
                 gqn_ref, gqr_ref, gkn_ref, gkr_ref, qn_ref, qr_ref, k_ref, v_ref):
    cqn = _rms(cq_ref[...].astype(F32), gqa_ref[...]).astype(BF16)
    q = jnp.dot(cqn, wqb_ref[...], preferred_element_type=F32)
    ckvn = _rms(ckv_ref[...].astype(F32), gkva_ref[...]).astype(BF16)
    kv = jnp.dot(ckvn, wkvb_ref[...], preferred_element_type=F32)
    kr2 = kr_ref[...].astype(F32)

    lane = lax.broadcasted_iota(jnp.int32, (1, LANES), 1)
    lo = (lane % ROPE_DIM) < (ROPE_DIM // 2)
    cos = cos_ref[...]
    sin = sin_ref[...]
    gqn = gqn_ref[...]
    gqr = gqr_ref[...]
    gkn = gkn_ref[...]
    gkr = gkr_ref[...]
    inv_d = 1.0 / QK_DIM
    nope_all = MLA_HEADS * NOPE_DIM

    for j in range(MLA_HEADS // 2):
        rp = q[:, nope_all + j * LANES: nope_all + (j + 1) * LANES]
        rp2 = rp * rp
        s_pair = (jnp.sum(jnp.where(lo, rp2, 0.0), axis=-1, keepdims=True),
                  jnp.sum(jnp.where(lo, 0.0, rp2), axis=-1, keepdims=True))
        r_pair = []
        for e in range(2):
            h = 2 * j + e
            nh = q[:, h * NOPE_DIM:(h + 1) * NOPE_DIM]
            ss = jnp.sum(nh * nh, axis=-1, keepdims=True) + s_pair[e]
            r_h = lax.rsqrt(ss * inv_d + EPS)
            qn_ref[:, h * NOPE_DIM:(h + 1) * NOPE_DIM] = (nh * r_h * gqn).astype(BF16)
            r_pair.append(r_h)
        rscale = jnp.where(lo, r_pair[0], r_pair[1])
        qr_ref[:, j * LANES:(j + 1) * LANES] = _rope(rp * rscale * gqr, cos, sin).astype(BF16)

    kr2sq = kr2 * kr2
    s_kr = jnp.sum(jnp.where(lo, kr2sq, 0.0), axis=-1, keepdims=True)
    rope_k = _rope(kr2 * gkr, cos, sin)
    hw = NOPE_DIM + V_DIM
    for h in range(MLA_HEADS):
        kn = kv[:, h * hw: h * hw + NOPE_DIM]
        vh = kv[:, h * hw + NOPE_DIM:(h + 1) * hw]
        ss = jnp.sum(kn * kn, axis=-1, keepdims=True) + s_kr
        r_h = lax.rsqrt(ss * inv_d + EPS)
        k_ref[:, h * hw: h * hw + NOPE_DIM] = (kn * r_h * gkn).astype(BF16)
        keep = lo if h % 2 == 0 else jnp.logical_not(lo)
        k_ref[:, h * hw + NOPE_DIM:(h + 1) * hw] = jnp.where(keep, rope_k * r_h, 0.0).astype(BF16)
        v_ref[:, h * V_DIM:(h + 1) * V_DIM] = vh.astype(BF16)


def _prep(u, cos, sin, gqa, gkva, wqb, wkvb, gqn, gqr, gkn, gkr, seq, tm):
    t = u.shape[0]
    nseq = seq // tm
    full = lambda a: pl.BlockSpec(a.shape, lambda i: (0,) * a.ndim)
    cq_blk = 3 * CONV_DIM // Q_RANK
    ckv_blk = (3 * CONV_DIM + Q_RANK) // KV_RANK
    kr_blk = (3 * CONV_DIM + Q_RANK + KV_RANK) // LANES
    return pl.pallas_call(
        _prep_kernel,
        out_shape=(
            jax.ShapeDtypeStruct((t, MLA_HEADS * NOPE_DIM), BF16),
            jax.ShapeDtypeStruct((t, MLA_HEADS * ROPE_DIM), BF16),
            jax.ShapeDtypeStruct((t, MLA_HEADS * (NOPE_DIM + V_DIM)), BF16),
            jax.ShapeDtypeStruct((t, MLA_HEADS * V_DIM), BF16),
        ),
        grid=(t // tm,),
        in_specs=[
            pl.BlockSpec((tm, Q_RANK), lambda i: (i, cq_blk)),
            pl.BlockSpec((tm, KV_RANK), lambda i: (i, ckv_blk)),
            pl.BlockSpec((tm, LANES), lambda i: (i, kr_blk)),
            pl.BlockSpec((tm, LANES), lambda i: (i % nseq, 0)),
            pl.BlockSpec((tm, LANES), lambda i: (i % nseq, 0)),
            full(gqa), full(gkva), full(wqb), full(wkvb), full(gqn), full(gqr), full(gkn), full(gkr),
        ],
        out_specs=(
            pl.BlockSpec((tm, MLA_HEADS * NOPE_DIM), lambda i: (i, 0)),
            pl.BlockSpec((tm, MLA_HEADS * ROPE_DIM), lambda i: (i, 0)),
            pl.BlockSpec((tm, MLA_HEADS * (NOPE_DIM + V_DIM)), lambda i: (i, 0)),
            pl.BlockSpec((tm, MLA_HEADS * V_DIM), lambda i: (i, 0)),
        ),
        compiler_params=_params("parallel"),
        name="mla_prep",
    )(u, u, u, cos, sin, gqa, gkva, wqb, wkvb, gqn, gqr, gkn, gkr)


def _flash_kernel(qn_ref, qr_ref, k_ref, v_ref, o_ref, *, sub, ck):
    tq = qn_ref.shape[0]
    seq = k_ref.shape[0]
    ones = jnp.ones((ck, LANES), BF16)
    vs = [jnp.concatenate([v_ref[c:c + ck, :], ones], axis=1) for c in range(0, seq, ck)]
    for r in range(0, tq, sub):
        q = jnp.concatenate([qn_ref[r:r + sub, :], qr_ref[r:r + sub, :]], axis=1)
        m = acc = None
        for ci, c in enumerate(range(0, seq, ck)):
            s = lax.dot_general(q, k_ref[c:c + ck, :], (((1,), (1,)), ((), ())), preferred_element_type=F32)
            mc = jnp.max(s, axis=-1, keepdims=True)
            m_new = mc if m is None else jnp.maximum(m, mc)
            p = jnp.exp(s - m_new)
            pv = jnp.dot(p.astype(BF16), vs[ci], preferred_element_type=F32)
            acc = pv if m is None else jnp.exp(m - m_new) * acc + pv
            m = m_new
        o_ref[r:r + sub, :] = (acc[:, :V_DIM] / acc[:, V_DIM:V_DIM + 1]).astype(o_ref.dtype)


def _flash(qn, qr, k, v, batch, seq, tq, sub, ck):
    t = qn.shape[0]
    nq = seq // tq
    return pl.pallas_call(
        functools.partial(_flash_kernel, sub=sub, ck=ck),
        out_shape=jax.ShapeDtypeStruct((t, MLA_HEADS * V_DIM), BF16),
        grid=(batch, MLA_HEADS, nq),
        in_specs=[
            pl.BlockSpec((tq, NOPE_DIM), lambda b, h, i: (b * nq + i, h)),
            pl.BlockSpec((tq, LANES), lambda b, h, i: (b * nq + i, h // 2)),
            pl.BlockSpec((seq, NOPE_DIM + V_DIM), lambda b, h, i: (b, h)),
            pl.BlockSpec((seq, V_DIM), lambda b, h, i: (b, h)),
        ],
        out_specs=pl.BlockSpec((tq, V_DIM), lambda b, h, i: (b * nq + i, h)),
        compiler_params=_params("parallel", "parallel", "arbitrary"),
        name="flash",
    )(qn, qr, k, v)


def _outproj_kernel(x_ref, xb_ref, xc_ref, xi_ref, xcp_ref, xip_ref, xcn_ref, xin_ref, cw_ref, att_ref,
                    wo_ref, o_ref, *, tm, seq):
    pos0 = lax.rem(pl.program_id(0) * tm, seq)
    u = xc_ref[...].astype(F32) * xi_ref[...].astype(F32)
    halo_p = xcp_ref[...].astype(F32) * xip_ref[...].astype(F32)
    halo_n = xcn_ref[...].astype(F32) * xin_ref[...].astype(F32)
    u_before = jnp.where(pos0 != 0, halo_p[BF16_SUBLANES - 1:BF16_SUBLANES, :], 0.0)
    u_after = jnp.where(pos0 + tm != seq, halo_n[0:1, :], 0.0)
    row = lax.broadcasted_iota(jnp.int32, (tm, 1), 0)
    u_m1 = jnp.where(row == 0, u_before, pltpu.roll(u, 1, 0))
    u_p1 = jnp.where(row == tm - 1, u_after, pltpu.roll(u, tm - 1, 0))
    cw = cw_ref[...]
    conv = cw[0:1, :] * u_m1 + cw[1:2, :] * u + cw[2:3, :] * u_p1
    a = (xb_ref[...].astype(F32) * conv).astype(BF16)
    y = jnp.dot(a, wo_ref[0:CONV_DIM, :], preferred_element_type=F32)
    y = y + jnp.dot(att_ref[...], wo_ref[CONV_DIM:, :], preferred_element_type=F32)
    o_ref[...] = x_ref[...] + y


def _outproj(x, u, cw, att, wo, seq, tm):
    t, d = x.shape
    rb = tm // BF16_SUBLANES
    last = t // BF16_SUBLANES - 1
    prev_map = lambda c: (lambda i: (jnp.maximum(i * rb - 1, 0), c))
    next_map = lambda c: (lambda i: (jnp.minimum((i + 1) * rb, last), c))
    return pl.pallas_call(
        functools.partial(_outproj_kernel, tm=tm, seq=seq),
        out_shape=jax.ShapeDtypeStruct((t, d), F32),
        grid=(t // tm,),
        in_specs=[
            pl.BlockSpec((tm, d), lambda i: (i, 0)),
            pl.BlockSpec((tm, CONV_DIM), lambda i: (i, 0)),
            pl.BlockSpec((tm, CONV_DIM), lambda i: (i, 1)),
            pl.BlockSpec((tm, CONV_DIM), lambda i: (i, 2)),
            pl.BlockSpec((BF16_SUBLANES, CONV_DIM), prev_map(1)),
            pl.BlockSpec((BF16_SUBLANES, CONV_DIM), prev_map(2)),
            pl.BlockSpec((BF16_SUBLANES, CONV_DIM), next_map(1)),
            pl.BlockSpec((BF16_SUBLANES, CONV_DIM), next_map(2)),
            pl.BlockSpec(cw.shape, lambda i: (0, 0)),
            pl.BlockSpec((tm, MLA_HEADS * V_DIM), lambda i: (i, 0)),
            pl.BlockSpec(wo.shape, lambda i: (0, 0)),
        ],
        out_specs=pl.BlockSpec((tm, d), lambda i: (i, 0)),
        compiler_params=_params("parallel"),
        name="outproj",
    )(x, u, u, u, u, u, u, u, cw, att, wo)


def _pool_kernel(x_ref, xp_ref, xn_ref, g_ref, wp_ref, ps_ref, o_ref, *, tm, seq):
    pos0 = lax.rem(pl.program_id(0) * tm, seq)
    x = x_ref[...]
    g = g_ref[...]
    xp = jnp.where(pos0 != 0, xp_ref[...], 0.0)
    xn = jnp.where(pos0 + tm != seq, xn_ref[...], 0.0)
    hext = _rms(jnp.concatenate([xp, x, xn], axis=0), g)
    n = tm + 2 * POOL_HALO
    pos = pos0 + lax.broadcasted_iota(jnp.int32, (tm, 1), 0)
    ps = ps_ref[...]
    for gi, w in enumerate(POOL_WINDOWS):
        c0, c1 = gi * POOL_GROUP, (gi + 1) * POOL_GROUP
        hg = hext[:, c0:c1]
        half = w // 2
        run = hg
        step = 1
        while step < half:
            run = run + pltpu.roll(run, n - step, 0)
            step *= 2
        win = run + pltpu.roll(run, half, 0)
        win = win[POOL_HALO:POOL_HALO + tm, :]
        cnt = jnp.minimum(pos + half, seq) - jnp.maximum(pos - half, 0)
        mean = win * (1.0 / cnt.astype(F32))
        p = (mean - hg[POOL_HALO:POOL_HALO + tm, :]).astype(BF16)
        y = jnp.dot(p, wp_ref[gi], preferred_element_type=F32)
        o_ref[:, c0:c1] = x[:, c0:c1] + y * ps[:, c0:c1]


def _pool(x, g, wp, ps, seq, tm):
    t, d = x.shape
    rb = tm // POOL_HALO
    last = t // POOL_HALO - 1
    return pl.pallas_call(
        functools.partial(_pool_kernel, tm=tm, seq=seq),
        out_shape=jax.ShapeDtypeStruct((t, d), F32),
        grid=(t // tm,),
        in_specs=[
            pl.BlockSpec((tm, d), lambda i: (i, 0)),
            pl.BlockSpec((POOL_HALO, d), lambda i: (jnp.maximum(i * rb - 1, 0), 0)),
            pl.BlockSpec((POOL_HALO, d), lambda i: (jnp.minimum((i + 1) * rb, last), 0)),
            pl.BlockSpec((1, d), lambda i: (0, 0)),
            pl.BlockSpec(wp.shape, lambda i: (0, 0, 0)),
            pl.BlockSpec((1, d), lambda i: (0, 0)),
        ],
        out_specs=pl.BlockSpec((tm, d), lambda i: (i, 0)),
        compiler_params=_params("parallel"),
        name="pool",
    )(x, x, x, g, wp, ps)


def _mlp_kernel(x_ref, g_ref, wu_ref, wd_ref, o_ref, h_ref):
    @pl.when(pl.program_id(1) == 0)
    def _():
        x = x_ref[...]
        h_ref[...] = _rms(x, g_ref[...]).astype(BF16)
        o_ref[...] = x

    a = jnp.dot(h_ref[...], wu_ref[...], preferred_element_type=F32)
    a = jnp.square(jnp.maximum(a, 0.0)).astype(BF16)
    o_ref[...] += jnp.dot(a, wd_ref[...], preferred_element_type=F32)


def _mlp(x, g, wu, wd, layer, tm, tf):
    t, d = x.shape
    f = wu.shape[2]
    return pl.pallas_call(
        _mlp_kernel,
        out_shape=jax.ShapeDtypeStruct((t, d), F32),
        grid=(t // tm, f // tf),
        in_specs=[
            pl.BlockSpec((tm, d), lambda i, j: (i, 0)),
            pl.BlockSpec((None, 1, d), lambda i, j: (layer, 0, 0)),
            pl.BlockSpec((None, d, tf), lambda i, j: (layer, 0, j)),
            pl.BlockSpec((None, tf, d), lambda i, j: (layer, j, 0)),
        ],
        out_specs=pl.BlockSpec((tm, d), lambda i, j: (i, 0)),
        scratch_shapes=[pltpu.VMEM((tm, d), BF16)],
        compiler_params=_params("parallel", "arbitrary"),
        name="mlp",
    )(x, g, wu, wd)


def _rope_tables(seq):
    inv = 1.0 / (ROPE_THETA ** (jnp.arange(0, ROPE_DIM, 2, dtype=F32) / ROPE_DIM))
    ang = jnp.arange(seq, dtype=F32)[:, None] * inv[None, :]
    cos, sin = jnp.cos(ang), jnp.sin(ang)
    reps = LANES // (ROPE_DIM // 2)
    sign = jnp.concatenate([-jnp.ones((LANES // 2,), F32), jnp.ones((LANES // 2,), F32)])
    return jnp.tile(cos, (1, reps)), jnp.tile(sin, (1, reps)) * sign[None, :]


def _pair_lanes(g):
    return jnp.repeat(g.reshape(2, ROPE_DIM // 2), LANES // ROPE_DIM, axis=0).reshape(1, LANES)


def _prepare_even(norm_mix0, w_in0, conv_w, q_a_norm, w_qb, kv_a_norm, w_kvb, q_norm, k_norm, w_o0):
    half = ROPE_DIM // 2
    base = IN0_DIM - ROPE_DIM
    w16 = w_in0.astype(BF16)
    x1, x2 = w16[:, base:base + half], w16[:, base + half:]
    pad = jnp.zeros((D_MODEL, IN0_PAD - base - LANES), BF16)
    w_in = jnp.concatenate([w16[:, :base], x1, x1, x2, x2, pad], axis=1)
    wq3 = w_qb.astype(BF16).reshape(Q_RANK, MLA_HEADS, QK_DIM)
    wq_rope = wq3[:, :, NOPE_DIM:].reshape(Q_RANK, MLA_HEADS // 2, 2, 2, half)
    wq_rope = wq_rope.transpose(0, 1, 3, 2, 4).reshape(Q_RANK, MLA_HEADS * ROPE_DIM)
    wqb = jnp.concatenate([wq3[:, :, :NOPE_DIM].reshape(Q_RANK, MLA_HEADS * NOPE_DIM), wq_rope], axis=1)
    scale = QK_DIM ** -0.5
    return dict(
        g0=norm_mix0.reshape(1, D_MODEL), w_in=w_in, cw=conv_w,
        gqa=q_a_norm.reshape(1, Q_RANK), gkva=kv_a_norm.reshape(1, KV_RANK),
        wqb=wqb, wkvb=w_kvb.astype(BF16),
        gqn=(q_norm[:NOPE_DIM] * scale).reshape(1, NOPE_DIM),
        gqr=_pair_lanes(q_norm[NOPE_DIM:] * scale),
        gkn=k_norm[:NOPE_DIM].reshape(1, NOPE_DIM),
        gkr=_pair_lanes(k_norm[NOPE_DIM:]),
        wo=w_o0.astype(BF16),
    )


def _tile(n, pref):
    return pref if n % pref == 0 else n


def _trunk(x3, even, odd, mlp, cos, sin):
    batch, seq, d = x3.shape
    t = batch * seq
    x = x3.reshape(t, d)
    tm_mlp = _tile(seq, 1024)
    tm = _tile(seq, 512)
    gm, wu, wd = mlp
    for i in range(gm.shape[0]):
        j = i // 2
        if i % 2 == 0:
            p = even[j]
            u = _inproj(x, p["g0"], p["w_in"], tm_mlp, 1024)
            qn, qr, k, v = _prep(u, cos, sin, p["gqa"], p["gkva"], p["wqb"], p["wkvb"],
                                 p["gqn"], p["gqr"], p["gkn"], p["gkr"], seq, tm)
            att = _flash(qn, qr, k, v, batch, seq, seq if seq <= 2048 else 1024, 256, _tile(seq, 2048))
            x = _outproj(x, u, p["cw"], att, p["wo"], seq, tm)
        else:
            g1, wp, ps = odd[j]
            x = _pool(x, g1, wp, ps, seq, tm)
        x = _mlp(x, gm, wu, wd, i, tm_mlp, 512)
    return x.reshape(batch, seq, d)


def kernel(x_prompt, x_sample, norm_mix0, w_in0, conv_w, q_a_norm, w_qb, kv_a_norm, w_kvb, q_norm, k_norm, w_o0,
           norm_mix1, w_pool, pool_scale, norm_mlp, w_up, w_down):
    depth = norm_mlp.shape[0]
    even = [_prepare_even(norm_mix0[j], w_in0[j], conv_w[j], q_a_norm[j], w_qb[j], kv_a_norm[j], w_kvb[j],
                          q_norm[j], k_norm[j], w_o0[j]) for j in range(norm_mix0.shape[0])]
    odd = [(norm_mix1[j].reshape(1, D_MODEL), w_pool[j].astype(BF16), pool_scale[j].reshape(1, D_MODEL))
           for j in range(norm_mix1.shape[0])]
    mlp = (norm_mlp.reshape(depth, 1, D_MODEL), w_up.astype(BF16), w_down.astype(BF16))
    outs = []
    for x3 in (x_prompt, x_sample):
        cos, sin = _rope_tables(x3.shape[1])
        outs.append(_trunk(x3, even, odd, mlp, cos, sin))
    return tuple(outs)
```

```python
import functools

import jax
import jax.numpy as jnp
from jax import lax
from jax.experimental import pallas as pl
from jax.experimental.pallas import tpu as pltpu

F32 = jnp.float32
BF16 = jnp.bfloat16

D_MODEL = 2048
CONV_DIM = D_MODEL // 2
MLA_HEADS = 8
NOPE_DIM = 128
ROPE_DIM = 64
V_DIM = 128
QK_DIM = NOPE_DIM + ROPE_DIM
Q_RANK = D_MODEL // 4
KV_RANK = D_MODEL // 8
IN0_DIM = 3 * CONV_DIM + Q_RANK + KV_RANK + ROPE_DIM
IN0_PAD = 4096
POOL_WINDOWS = (2, 4, 8, 16)
POOL_GROUP = D_MODEL // len(POOL_WINDOWS)
POOL_HALO = 8
D_FF = 4 * D_MODEL
ROPE_THETA = 10000.0
EPS = 1e-6

LANES = 128
BF16_SUBLANES = 16
VMEM_LIMIT = 56 * 1024 * 1024
MLP_VMEM_LIMIT = 58 * 1024 * 1024


def _params(*sem, vmem=VMEM_LIMIT):
    return pltpu.CompilerParams(dimension_semantics=sem, vmem_limit_bytes=vmem)


def _rms(x, g):
    ms = jnp.mean(x * x, axis=-1, keepdims=True)
    return x * lax.rsqrt(ms + EPS) * g


def _inproj_kernel(x_ref, g_ref, w_ref, o_ref, h_ref):
    def project(h):
        return jnp.dot(h, w_ref[...], preferred_element_type=F32).astype(o_ref.dtype)

    @pl.when(pl.program_id(1) == 0)
    def _():
        h = _rms(x_ref[...], g_ref[...]).astype(BF16)
        h_ref[...] = h
        o_ref[...] = project(h)

    @pl.when(pl.program_id(1) != 0)
    def _():
        o_ref[...] = project(h_ref[...])


def _inproj(x, g, w, tm, tn):
    t, d = x.shape
    n = w.shape[1]
    return pl.pallas_call(
        _inproj_kernel,
        out_shape=jax.ShapeDtypeStruct((t, n), BF16),
        grid=(t // tm, n // tn),
        in_specs=[
            pl.BlockSpec((tm, d), lambda i, j: (i, 0)),
            pl.BlockSpec((1, d), lambda i, j: (0, 0)),
            pl.BlockSpec((d, tn), lambda i, j: (0, j)),
        ],
        out_specs=pl.BlockSpec((tm, tn), lambda i, j: (i, j)),
        scratch_shapes=[pltpu.VMEM((tm, d), BF16)],
        compiler_params=_params("parallel", "arbitrary"),
        name="inproj",
    )(x, g, w)


def _rope(r, cos, sin_signed):
    return r * cos + pltpu.roll(r, LANES // 2, 1) * sin_signed


def _prep_kernel(cq_ref, ckv_ref, kr_ref, cos_ref, sin_ref, gqa_ref, gkva_ref, wqb_ref, wkvb_ref,
                 gqn_ref, gqr_ref, gkn_ref, gkr_ref, qn_ref, qr_ref, k_ref, v_ref):
    cqn = _rms(cq_ref[...].astype(F32), gqa_ref[...]).astype(BF16)
    q = jnp.dot(cqn, wqb_ref[...], preferred_element_type=F32)
    ckvn = _rms(ckv_ref[...].astype(F32), gkva_ref[...]).astype(BF16)
    kv = jnp.dot(ckvn, wkvb_ref[...], preferred_element_type=F32)
    kr2 = kr_ref[...].astype(F32)

    lane = lax.broadcasted_iota(jnp.int32, (1, LANES), 1)
    lo = (lane % ROPE_DIM) < (ROPE_DIM // 2)
    cos = cos_ref[...]
    sin = sin_ref[...]
    gqn = gqn_ref[...]
    gqr = gqr_ref[...]
    gkn = gkn_ref[...]
    gkr = gkr_ref[...]
    inv_d = 1.0 / QK_DIM
    nope_all = MLA_HEADS * NOPE_DIM

    for j in range(MLA_HEADS // 2):
        rp = q[:, nope_all + j * LANES: nope_all + (j + 1) * LANES]
        rp2 = rp * rp
        rp2_pair = (jnp.where(lo, rp2, 0.0), jnp.where(lo, 0.0, rp2))
        r_pair = []
        for e in range(2):
            h = 2 * j + e
            nh = q[:, h * NOPE_DIM:(h + 1) * NOPE_DIM]
            ss = jnp.sum(nh * nh + rp2_pair[e], axis=-1, keepdims=True)
            r_h = lax.rsqrt(ss * inv_d + EPS)
            qn_ref[:, h * NOPE_DIM:(h + 1) * NOPE_DIM] = (nh * r_h * gqn).astype(BF16)
            r_pair.append(r_h)
        rscale = jnp.where(lo, r_pair[0], r_pair[1])
        qr_ref[:, j * LANES:(j + 1) * LANES] = _rope(rp * rscale * gqr, cos, sin).astype(BF16)

    kr_sq = jnp.where(lo, kr2 * kr2, 0.0)
    rope_k = _rope(kr2 * gkr, cos, sin)
    hw = NOPE_DIM + V_DIM
    for h in range(MLA_HEADS):
        kn = kv[:, h * hw: h * hw + NOPE_DIM]
        vh = kv[:, h * hw + NOPE_DIM:(h + 1) * hw]
        ss = jnp.sum(kn * kn + kr_sq, axis=-1, keepdims=True)
        r_h = lax.rsqrt(ss * inv_d + EPS)
        k_ref[:, h * hw: h * hw + NOPE_DIM] = (kn * r_h * gkn).astype(BF16)
        keep = lo if h % 2 == 0 else jnp.logical_not(lo)
        k_ref[:, h * hw + NOPE_DIM:(h + 1) * hw] = jnp.where(keep, rope_k * r_h, 0.0).astype(BF16)
        v_ref[:, h * V_DIM:(h + 1) * V_DIM] = vh.astype(BF16)


def _prep(u, cos, sin, gqa, gkva, wqb, wkvb, gqn, gqr, gkn, gkr, seq, tm):
    t = u.shape[0]
    nseq = seq // tm
    full = lambda a: pl.BlockSpec(a.shape, lambda i: (0,) * a.ndim)
    cq_blk = 3 * CONV_DIM // Q_RANK
    ckv_blk = (3 * CONV_DIM + Q_RANK) // KV_RANK
    kr_blk = (3 * CONV_DIM + Q_RANK + KV_RANK) // LANES
    return pl.pallas_call(
        _prep_kernel,
        out_shape=(
            jax.ShapeDtypeStruct((t, MLA_HEADS * NOPE_DIM), BF16),
            jax.ShapeDtypeStruct((t, MLA_HEADS * ROPE_DIM), BF16),
            jax.ShapeDtypeStruct((t, MLA_HEADS * (NOPE_DIM + V_DIM)), BF16),
            jax.ShapeDtypeStruct((t, MLA_HEADS * V_DIM), BF16),
        ),
        grid=(t // tm,),
        in_specs=[
            pl.BlockSpec((tm, Q_RANK), lambda i: (i, cq_blk)),
            pl.BlockSpec((tm, KV_RANK), lambda i: (i, ckv_blk)),
            pl.BlockSpec((tm, LANES), lambda i: (i, kr_blk)),
            pl.BlockSpec((tm, LANES), lambda i: (i % nseq, 0)),
            pl.BlockSpec((tm, LANES), lambda i: (i % nseq, 0)),
            full(gqa), full(gkva), full(wqb), full(wkvb), full(gqn), full(gqr), full(gkn), full(gkr),
        ],
        out_specs=(
            pl.BlockSpec((tm, MLA_HEADS * NOPE_DIM), lambda i: (i, 0)),
            pl.BlockSpec((tm, MLA_HEADS * ROPE_DIM), lambda i: (i, 0)),
            pl.BlockSpec((tm, MLA_HEADS * (NOPE_DIM + V_DIM)), lambda i: (i, 0)),
            pl.BlockSpec((tm, MLA_HEADS * V_DIM), lambda i: (i, 0)),
        ),
        compiler_params=_params("parallel"),
        name="mla_prep",
    )(u, u, u, cos, sin, gqa, gkva, wqb, wkvb, gqn, gqr, gkn, gkr)


def _flash_kernel(qn_ref, qr_ref, k_ref, v_ref, o_ref, *, sub, ck):
    tq = qn_ref.shape[0]
    seq = k_ref.shape[0]
    ones = jnp.ones((ck, LANES), BF16)
    vs = [jnp.concatenate([v_ref[c:c + ck, :], ones], axis=1) for c in range(0, seq, ck)]
    for r in range(0, tq, sub):
        q = jnp.concatenate([qn_ref[r:r + sub, :], qr_ref[r:r + sub, :]], axis=1)
        m = acc = None
        for ci, c in enumerate(range(0, seq, ck)):
            s = lax.dot_general(q, k_ref[c:c + ck, :], (((1,), (1,)), ((), ())), preferred_element_type=F32)
            mc = jnp.max(s, axis=-1, keepdims=True)
            m_new = mc if m is None else jnp.maximum(m, mc)
            p = jnp.exp(s - m_new)
            pv = jnp.dot(p.astype(BF16), vs[ci], preferred_element_type=F32)
            acc = pv if m is None else jnp.exp(m - m_new) * acc + pv
            m = m_new
        o_ref[r:r + sub, :] = (acc[:, :V_DIM] / acc[:, V_DIM:V_DIM + 1]).astype(o_ref.dtype)


def _flash(qn, qr, k, v, batch, seq, tq, sub, ck):
    t = qn.shape[0]
    nq = seq // tq
    return pl.pallas_call(
        functools.partial(_flash_kernel, sub=sub, ck=ck),
        out_shape=jax.ShapeDtypeStruct((t, MLA_HEADS * V_DIM), BF16),
        grid=(batch, MLA_HEADS, nq),
        in_specs=[
            pl.BlockSpec((tq, NOPE_DIM), lambda b, h, i: (b * nq + i, h)),
            pl.BlockSpec((tq, LANES), lambda b, h, i: (b * nq + i, h // 2)),
            pl.BlockSpec((seq, NOPE_DIM + V_DIM), lambda b, h, i: (b, h)),
            pl.BlockSpec((seq, V_DIM), lambda b, h, i: (b, h)),
        ],
        out_specs=pl.BlockSpec((tq, V_DIM), lambda b, h, i: (b * nq + i, h)),
        compiler_params=_params("parallel", "parallel", "arbitrary"),
        name="flash",
    )(qn, qr, k, v)


def _outproj_kernel(x_ref, xb_ref, xc_ref, xi_ref, xcp_ref, xip_ref, xcn_ref, xin_ref, cw_ref, att_ref,
                    wo_ref, o_ref, *, tm, seq):
    pos0 = lax.rem(pl.program_id(0) * tm, seq)
    u = xc_ref[...].astype(F32) * xi_ref[...].astype(F32)
    halo_p = xcp_ref[...].astype(F32) * xip_ref[...].astype(F32)
    halo_n = xcn_ref[...].astype(F32) * xin_ref[...].astype(F32)
    u_before = jnp.where(pos0 != 0, halo_p[BF16_SUBLANES - 1:BF16_SUBLANES, :], 0.0)
    u_after = jnp.where(pos0 + tm != seq, halo_n[0:1, :], 0.0)
    row = lax.broadcasted_iota(jnp.int32, (tm, 1), 0)
    u_m1 = jnp.where(row == 0, u_before, pltpu.roll(u, 1, 0))
    u_p1 = jnp.where(row == tm - 1, u_after, pltpu.roll(u, tm - 1, 0))
    cw = cw_ref[...]
    conv = cw[0:1, :] * u_m1 + cw[1:2, :] * u + cw[2:3, :] * u_p1
    a = (xb_ref[...].astype(F32) * conv).astype(BF16)
    y = jnp.dot(a, wo_ref[0:CONV_DIM, :], preferred_element_type=F32)
    y = y + jnp.dot(att_ref[...], wo_ref[CONV_DIM:, :], preferred_element_type=F32)
    o_ref[...] = x_ref[...] + y


def _outproj(x, u, cw, att, wo, seq, tm):
    t, d = x.shape
    rb = tm // BF16_SUBLANES
    last = t // BF16_SUBLANES - 1
    prev_map = lambda c: (lambda i: (jnp.maximum(i * rb - 1, 0), c))
    next_map = lambda c: (lambda i: (jnp.minimum((i + 1) * rb, last), c))
    return pl.pallas_call(
        functools.partial(_outproj_kernel, tm=tm, seq=seq),
        out_shape=jax.ShapeDtypeStruct((t, d), F32),
        grid=(t // tm,),
        in_specs=[
            pl.BlockSpec((tm, d), lambda i: (i, 0)),
            pl.BlockSpec((tm, CONV_DIM), lambda i: (i, 0)),
            pl.BlockSpec((tm, CONV_DIM), lambda i: (i, 1)),
            pl.BlockSpec((tm, CONV_DIM), lambda i: (i, 2)),
            pl.BlockSpec((BF16_SUBLANES, CONV_DIM), prev_map(1)),
            pl.BlockSpec((BF16_SUBLANES, CONV_DIM), prev_map(2)),
            pl.BlockSpec((BF16_SUBLANES, CONV_DIM), next_map(1)),
            pl.BlockSpec((BF16_SUBLANES, CONV_DIM), next_map(2)),
            pl.BlockSpec(cw.shape, lambda i: (0, 0)),
            pl.BlockSpec((tm, MLA_HEADS * V_DIM), lambda i: (i, 0)),
            pl.BlockSpec(wo.shape, lambda i: (0, 0)),
        ],
        out_specs=pl.BlockSpec((tm, d), lambda i: (i, 0)),
        compiler_params=_params("parallel"),
        name="outproj",
    )(x, u, u, u, u, u, u, u, cw, att, wo)


def _pool_kernel(x_ref, xp_ref, xn_ref, g_ref, wp_ref, ps_ref, o_ref, *, tm, seq):
    pos0 = lax.rem(pl.program_id(0) * tm, seq)
    x = x_ref[...]
    g = g_ref[...]
    xp = jnp.where(pos0 != 0, xp_ref[...], 0.0)
    xn = jnp.where(pos0 + tm != seq, xn_ref[...], 0.0)
    hext = _rms(jnp.concatenate([xp, x, xn], axis=0), g)
    n = tm + 2 * POOL_HALO
    pos = pos0 + lax.broadcasted_iota(jnp.int32, (tm, 1), 0)
    ps = ps_ref[...]
    for gi, w in enumerate(POOL_WINDOWS):
        c0, c1 = gi * POOL_GROUP, (gi + 1) * POOL_GROUP
        hg = hext[:, c0:c1]
        half = w // 2
        run = hg
        step = 1
        while step < half:
            run = run + pltpu.roll(run, n - step, 0)
            step *= 2
        win = run + pltpu.roll(run, half, 0)
        win = win[POOL_HALO:POOL_HALO + tm, :]
        cnt = jnp.minimum(pos + half, seq) - jnp.maximum(pos - half, 0)
        mean = win * (1.0 / cnt.astype(F32))
        p = (mean - hg[POOL_HALO:POOL_HALO + tm, :]).astype(BF16)
        y = jnp.dot(p, wp_ref[gi], preferred_element_type=F32)
        o_ref[:, c0:c1] = x[:, c0:c1] + y * ps[:, c0:c1]


def _pool(x, g, wp, ps, seq, tm):
    t, d = x.shape
    rb = tm // POOL_HALO
    last = t // POOL_HALO - 1
    return pl.pallas_call(
        functools.partial(_pool_kernel, tm=tm, seq=seq),
        out_shape=jax.ShapeDtypeStruct((t, d), F32),
        grid=(t // tm,),
        in_specs=[
            pl.BlockSpec((tm, d), lambda i: (i, 0)),
            pl.BlockSpec((POOL_HALO, d), lambda i: (jnp.maximum(i * rb - 1, 0), 0)),
            pl.BlockSpec((POOL_HALO, d), lambda i: (jnp.minimum((i + 1) * rb, last), 0)),
            pl.BlockSpec((1, d), lambda i: (0, 0)),
            pl.BlockSpec(wp.shape, lambda i: (0, 0, 0)),
            pl.BlockSpec((1, d), lambda i: (0, 0)),
        ],
        out_specs=pl.BlockSpec((tm, d), lambda i: (i, 0)),
        compiler_params=_params("parallel"),
        name="pool",
    )(x, x, x, g, wp, ps)


def _mlp_kernel(x_ref, g_ref, wu_ref, wd_ref, o_ref, h_ref):
    def down(h):
        a = jnp.dot(h, wu_ref[...], preferred_element_type=F32)
        a = jnp.square(jnp.maximum(a, 0.0)).astype(BF16)
        return jnp.dot(a, wd_ref[...], preferred_element_type=F32)

    @pl.when(pl.program_id(1) == 0)
    def _():
        x = x_ref[...]
        h = _rms(x, g_ref[...]).astype(BF16)
        h_ref[...] = h
        o_ref[...] = x + down(h)

    @pl.when(pl.program_id(1) != 0)
    def _():
        o_ref[...] += down(h_ref[...])


def _mlp(x, g, wu, wd, layer, tm, tf):
    t, d = x.shape
    f = wu.shape[2]
    return pl.pallas_call(
        _mlp_kernel,
        out_shape=jax.ShapeDtypeStruct((t, d), F32),
        grid=(t // tm, f // tf),
        in_specs=[
            pl.BlockSpec((tm, d), lambda i, j: (i, 0)),
            pl.BlockSpec((None, 1, d), lambda i, j: (layer, 0, 0)),
            pl.BlockSpec((None, d, tf), lambda i, j: (layer, 0, j)),
            pl.BlockSpec((None, tf, d), lambda i, j: (layer, j, 0)),
        ],
        out_specs=pl.BlockSpec((tm, d), lambda i, j: (i, 0)),
        scratch_shapes=[pltpu.VMEM((tm, d), BF16)],
        compiler_params=_params("parallel", "arbitrary", vmem=MLP_VMEM_LIMIT),
        name="mlp",
    )(x, g, wu, wd)


def _rope_tables(seq):
    inv = 1.0 / (ROPE_THETA ** (jnp.arange(0, ROPE_DIM, 2, dtype=F32) / ROPE_DIM))
    ang = jnp.arange(seq, dtype=F32)[:, None] * inv[None, :]
    cos, sin = jnp.cos(ang), jnp.sin(ang)
    reps = LANES // (ROPE_DIM // 2)
    sign = jnp.concatenate([-jnp.ones((LANES // 2,), F32), jnp.ones((LANES // 2,), F32)])
    return jnp.tile(cos, (1, reps)), jnp.tile(sin, (1, reps)) * sign[None, :]


def _pair_lanes(g):
    return jnp.repeat(g.reshape(2, ROPE_DIM // 2), LANES // ROPE_DIM, axis=0).reshape(1, LANES)


def _prepare_even(norm_mix0, w_in0, conv_w, q_a_norm, w_qb, kv_a_norm, w_kvb, q_norm, k_norm, w_o0):
    half = ROPE_DIM // 2
    base = IN0_DIM - ROPE_DIM
    w16 = w_in0.astype(BF16)
    x1, x2 = w16[:, base:base + half], w16[:, base + half:]
    pad = jnp.zeros((D_MODEL, IN0_PAD - base - LANES), BF16)
    w_in = jnp.concatenate([w16[:, :base], x1, x1, x2, x2, pad], axis=1)
    wq3 = w_qb.astype(BF16).reshape(Q_RANK, MLA_HEADS, QK_DIM)
    wq_rope = wq3[:, :, NOPE_DIM:].reshape(Q_RANK, MLA_HEADS // 2, 2, 2, half)
    wq_rope = wq_rope.transpose(0, 1, 3, 2, 4).reshape(Q_RANK, MLA_HEADS * ROPE_DIM)
    wqb = jnp.concatenate([wq3[:, :, :NOPE_DIM].reshape(Q_RANK, MLA_HEADS * NOPE_DIM), wq_rope], axis=1)
    scale = QK_DIM ** -0.5
    return dict(
        g0=norm_mix0.reshape(1, D_MODEL), w_in=w_in, cw=conv_w,
        gqa=q_a_norm.reshape(1, Q_RANK), gkva=kv_a_norm.reshape(1, KV_RANK),
        wqb=wqb, wkvb=w_kvb.astype(BF16),
        gqn=(q_norm[:NOPE_DIM] * scale).reshape(1, NOPE_DIM),
        gqr=_pair_lanes(q_norm[NOPE_DIM:] * scale),
        gkn=k_norm[:NOPE_DIM].reshape(1, NOPE_DIM),
        gkr=_pair_lanes(k_norm[NOPE_DIM:]),
        wo=w_o0.astype(BF16),
    )


def _tile(n, pref):
    return pref if n % pref == 0 else n


def _trunk(x3, even, odd, mlp, cos, sin):
    batch, seq, d = x3.shape
    t = batch * seq
    x = x3.reshape(t, d)
    tm_mlp = _tile(seq, 1024)
    tm = _tile(seq, 512)
    gm, wu, wd = mlp
    for i in range(gm.shape[0]):
        j = i // 2
        if i % 2 == 0:
            p = even[j]
            u = _inproj(x, p["g0"], p["w_in"], tm_mlp, 2048)
            qn, qr, k, v = _prep(u, cos, sin, p["gqa"], p["gkva"], p["wqb"], p["wkvb"],
                                 p["gqn"], p["gqr"], p["gkn"], p["gkr"], seq, tm)
            att = _flash(qn, qr, k, v, batch, seq, seq if seq <= 2048 else 1024, 256, _tile(seq, 2048))
            x = _outproj(x, u, p["cw"], att, p["wo"], seq, tm)
        else:
            g1, wp, ps = odd[j]
            x = _pool(x, g1, wp, ps, seq, tm)
        x = _mlp(x, gm, wu, wd, i, tm_mlp, 1024)
    return x.reshape(batch, seq, d)


def kernel(x_prompt, x_sample, norm_mix0, w_in0, conv_w, q_a_norm, w_qb, kv_a_norm, w_kvb, q_norm, k_norm, w_o0,
           norm_mix1, w_pool, pool_scale, norm_mlp, w_up, w_down):
    depth = norm_mlp.shape[0]
    even = [_prepare_even(norm_mix0[j], w_in0[j], conv_w[j], q_a_norm[j], w_qb[j], kv_a_norm[j], w_kvb[j],
                          q_norm[j], k_norm[j], w_o0[j]) for j in range(norm_mix0.shape[0])]
    odd = [(norm_mix1[j].reshape(1, D_MODEL), w_pool[j].astype(BF16), pool_scale[j].reshape(1, D_MODEL))
           for j in range(norm_mix1.shape[0])]
    mlp = (norm_mlp.reshape(depth, 1, D_MODEL), w_up.astype(BF16), w_down.astype(BF16))
    outs = []
    for x3 in (x_prompt, x_sample):
        cos, sin = _rope_tables(x3.shape[1])
        outs.append(_trunk(x3, even, odd, mlp, cos, sin))
    return tuple(outs)
```

```python
import functools

import jax
import jax.numpy as jnp
from jax import lax
from jax.experimental import pallas as pl
from jax.experimental.pallas import tpu as pltpu

F32 = jnp.float32
BF16 = jnp.bfloat16

D_MODEL = 2048
CONV_DIM = D_MODEL // 2
MLA_HEADS = 8
NOPE_DIM = 128
ROPE_DIM = 64
V_DIM = 128
QK_DIM = NOPE_DIM + ROPE_DIM
Q_RANK = D_MODEL // 4
KV_RANK = D_MODEL // 8
IN0_DIM = 3 * CONV_DIM + Q_RANK + KV_RANK + ROPE_DIM
IN0_PAD = 4096
POOL_WINDOWS = (2, 4, 8, 16)
POOL_GROUP = D_MODEL // len(POOL_WINDOWS)
POOL_HALO = 8
D_FF = 4 * D_MODEL
ROPE_THETA = 10000.0
EPS = 1e-6

LANES = 128
BF16_SUBLANES = 16
VMEM_LIMIT = 56 * 1024 * 1024
MLP_VMEM_LIMIT = 58 * 1024 * 1024


def _params(*sem, vmem=VMEM_LIMIT):
    return pltpu.CompilerParams(dimension_semantics=sem, vmem_limit_bytes=vmem)


def _rms(x, g):
    ms = jnp.mean(x * x, axis=-1, keepdims=True)
    return x * lax.rsqrt(ms + EPS) * g


def _inproj_kernel(x_ref, g_ref, w_ref, o_ref, h_ref):
    def project(h):
        return jnp.dot(h, w_ref[...], preferred_element_type=F32).astype(o_ref.dtype)

    @pl.when(pl.program_id(1) == 0)
    def _():
        h = _rms(x_ref[...], g_ref[...]).astype(BF16)
        h_ref[...] = h
        o_ref[...] = project(h)

    @pl.when(pl.program_id(1) != 0)
    def _():
        o_ref[...] = project(h_ref[...])


def _inproj(x, g, w, tm, tn):
    t, d = x.shape
    n = w.shape[1]
    return pl.pallas_call(
        _inproj_kernel,
        out_shape=jax.ShapeDtypeStruct((t, n), BF16),
        grid=(t // tm, n // tn),
        in_specs=[
            pl.BlockSpec((tm, d), lambda i, j: (i, 0)),
            pl.BlockSpec((1, d), lambda i, j: (0, 0)),
            pl.BlockSpec((d, tn), lambda i, j: (0, j)),
        ],
        out_specs=pl.BlockSpec((tm, tn), lambda i, j: (i, j)),
        scratch_shapes=[pltpu.VMEM((tm, d), BF16)],
        compiler_params=_params("parallel", "arbitrary"),
        name="inproj",
    )(x, g, w)


def _rope(r, cos, sin_signed):
    return r * cos + pltpu.roll(r, LANES // 2, 1) * sin_signed


def _prep_kernel(cq_ref, ckv_ref, kr_ref, cos_ref, sin_ref, gqa_ref, gkva_ref, wqb_ref, wkvb_ref,
                 gqn_ref, gqr_ref, gkn_ref, gkr_ref, qn_ref, qr_ref, k_ref, v_ref):
    cqn = _rms(cq_ref[...].astype(F32), gqa_ref[...]).astype(BF16)
    q = jnp.dot(cqn, wqb_ref[...], preferred_element_type=F32)
    ckvn = _rms(ckv_ref[...].astype(F32), gkva_ref[...]).astype(BF16)
    kv = jnp.dot(ckvn, wkvb_ref[...], preferred_element_type=F32)
    kr2 = kr_ref[...].astype(F32)

    lane = lax.broadcasted_iota(jnp.int32, (1, LANES), 1)
    lo = (lane % ROPE_DIM) < (ROPE_DIM // 2)
    cos = cos_ref[...]
    sin = sin_ref[...]
    gqn = gqn_ref[...]
    gqr = gqr_ref[...]
    gkn = gkn_ref[...]
    gkr = gkr_ref[...]
    inv_d = 1.0 / QK_DIM
    nope_all = MLA_HEADS * NOPE_DIM

    for j in range(MLA_HEADS // 2):
        rp = q[:, nope_all + j * LANES: nope_all + (j + 1) * LANES]
        rp2 = rp * rp
        rp2_pair = (jnp.where(lo, rp2, 0.0), jnp.where(lo, 0.0, rp2))
        r_pair = []
        for e in range(2):
            h = 2 * j + e
            nh = q[:, h * NOPE_DIM:(h + 1) * NOPE_DIM]
            ss = jnp.sum(nh * nh + rp2_pair[e], axis=-1, keepdims=True)
            r_h = lax.rsqrt(ss * inv_d + EPS)
            qn_ref[:, h * NOPE_DIM:(h + 1) * NOPE_DIM] = (nh * r_h * gqn).astype(BF16)
            r_pair.append(r_h)
        rscale = jnp.where(lo, r_pair[0], r_pair[1])
        qr_ref[:, j * LANES:(j + 1) * LANES] = _rope(rp * rscale * gqr, cos, sin).astype(BF16)

    kr_sq = jnp.where(lo, kr2 * kr2, 0.0)
    rope_k = _rope(kr2 * gkr, cos, sin)
    hw = NOPE_DIM + V_DIM
    for h in range(MLA_HEADS):
        kn = kv[:, h * hw: h * hw + NOPE_DIM]
        vh = kv[:, h * hw + NOPE_DIM:(h + 1) * hw]
        ss = jnp.sum(kn * kn + kr_sq, axis=-1, keepdims=True)
        r_h = lax.rsqrt(ss * inv_d + EPS)
        k_ref[:, h * hw: h * hw + NOPE_DIM] = (kn * r_h * gkn).astype(BF16)
        keep = lo if h % 2 == 0 else jnp.logical_not(lo)
        k_ref[:, h * hw + NOPE_DIM:(h + 1) * hw] = jnp.where(keep, rope_k * r_h, 0.0).astype(BF16)
        v_ref[:, h * V_DIM:(h + 1) * V_DIM] = vh.astype(BF16)


def _prep(u, cos, sin, gqa, gkva, wqb, wkvb, gqn, gqr, gkn, gkr, seq, tm):
    t = u.shape[0]
    nseq = seq // tm
    full = lambda a: pl.BlockSpec(a.shape, lambda i: (0,) * a.ndim)
    cq_blk = 3 * CONV_DIM // Q_RANK
    ckv_blk = (3 * CONV_DIM + Q_RANK) // KV_RANK
    kr_blk = (3 * CONV_DIM + Q_RANK + KV_RANK) // LANES
    return pl.pallas_call(
        _prep_kernel,
        out_shape=(
            jax.ShapeDtypeStruct((t, MLA_HEADS * NOPE_DIM), BF16),
            jax.ShapeDtypeStruct((t, MLA_HEADS * ROPE_DIM), BF16),
            jax.ShapeDtypeStruct((t, MLA_HEADS * (NOPE_DIM + V_DIM)), BF16),
            jax.ShapeDtypeStruct((t, MLA_HEADS * V_DIM), BF16),
        ),
        grid=(t // tm,),
        in_specs=[
            pl.BlockSpec((tm, Q_RANK), lambda i: (i, cq_blk)),
            pl.BlockSpec((tm, KV_RANK), lambda i: (i, ckv_blk)),
            pl.BlockSpec((tm, LANES), lambda i: (i, kr_blk)),
            pl.BlockSpec((tm, LANES), lambda i: (i % nseq, 0)),
            pl.BlockSpec((tm, LANES), lambda i: (i % nseq, 0)),
            full(gqa), full(gkva), full(wqb), full(wkvb), full(gqn), full(gqr), full(gkn), full(gkr),
        ],
        out_specs=(
            pl.BlockSpec((tm, MLA_HEADS * NOPE_DIM), lambda i: (i, 0)),
            pl.BlockSpec((tm, MLA_HEADS * ROPE_DIM), lambda i: (i, 0)),
            pl.BlockSpec((tm, MLA_HEADS * (NOPE_DIM + V_DIM)), lambda i: (i, 0)),
            pl.BlockSpec((tm, MLA_HEADS * V_DIM), lambda i: (i, 0)),
        ),
        compiler_params=_params("parallel"),
        name="mla_prep",
    )(u, u, u, cos, sin, gqa, gkva, wqb, wkvb, gqn, gqr, gkn, gkr)


def _flash_kernel(qn_ref, qr_ref, k_ref, v_ref, *rest, sub, ck, n_cast):
    src_refs, o_ref, dst_refs = rest[:n_cast], rest[n_cast], rest[n_cast + 1:]
    for src_ref, dst_ref in zip(src_refs, dst_refs):
        dst_ref[...] = src_ref[...].astype(dst_ref.dtype)
    tq = qn_ref.shape[0]
    seq = k_ref.shape[0]
    ones = jnp.ones((ck, LANES), BF16)
    vs = [jnp.concatenate([v_ref[c:c + ck, :], ones], axis=1) for c in range(0, seq, ck)]
    for r in range(0, tq, sub):
        q = jnp.concatenate([qn_ref[r:r + sub, :], qr_ref[r:r + sub, :]], axis=1)
        m = acc = None
        for ci, c in enumerate(range(0, seq, ck)):
            s = lax.dot_general(q, k_ref[c:c + ck, :], (((1,), (1,)), ((), ())), preferred_element_type=F32)
            mc = jnp.max(s, axis=-1, keepdims=True)
            m_new = mc if m is None else jnp.maximum(m, mc)
            p = jnp.exp(s - m_new)
            pv = jnp.dot(p.astype(BF16), vs[ci], preferred_element_type=F32)
            acc = pv if m is None else jnp.exp(m - m_new) * acc + pv
            m = m_new
        o_ref[r:r + sub, :] = (acc[:, :V_DIM] / acc[:, V_DIM:V_DIM + 1]).astype(o_ref.dtype)


CAST_COLS = 1024
CAST_MAX_ROWS = 512


def _flash_cast_rows(weights, steps):
    rows = [w.size // CAST_COLS for w in weights]
    ok = all(w.size % CAST_COLS == 0 and r % steps == 0 and (r // steps) % BF16_SUBLANES == 0
             and r // steps <= CAST_MAX_ROWS for w, r in zip(weights, rows))
    return [r // steps for r in rows] if ok else None


def _flash(qn, qr, k, v, batch, seq, tq, sub, ck, cast=()):
    t = qn.shape[0]
    nq = seq // tq
    step_rows = _flash_cast_rows(cast, batch * MLA_HEADS * nq) if cast else []
    flat = [w.reshape(-1, CAST_COLS) for w in cast]
    slab = lambda b, h, i: ((b * MLA_HEADS + h) * nq + i, 0)
    outs = pl.pallas_call(
        functools.partial(_flash_kernel, sub=sub, ck=ck, n_cast=len(cast)),
        out_shape=[jax.ShapeDtypeStruct((t, MLA_HEADS * V_DIM), BF16)]
        + [jax.ShapeDtypeStruct(f.shape, BF16) for f in flat],
        grid=(batch, MLA_HEADS, nq),
        in_specs=[
            pl.BlockSpec((tq, NOPE_DIM), lambda b, h, i: (b * nq + i, h)),
            pl.BlockSpec((tq, LANES), lambda b, h, i: (b * nq + i, h // 2)),
            pl.BlockSpec((seq, NOPE_DIM + V_DIM), lambda b, h, i: (b, h)),
            pl.BlockSpec((seq, V_DIM), lambda b, h, i: (b, h)),
        ] + [pl.BlockSpec((r, CAST_COLS), slab) for r in step_rows],
        out_specs=[pl.BlockSpec((tq, V_DIM), lambda b, h, i: (b * nq + i, h))]
        + [pl.BlockSpec((r, CAST_COLS), slab) for r in step_rows],
        compiler_params=_params("parallel", "parallel", "arbitrary"),
        name="flash",
    )(qn, qr, k, v, *flat)
    return [outs[0]] + [o.reshape(w.shape) for o, w in zip(outs[1:], cast)]


def _outproj_kernel(x_ref, xb_ref, xc_ref, xi_ref, xcp_ref, xip_ref, xcn_ref, xin_ref, cw_ref, att_ref,
                    wo_ref, o_ref, *, tm, seq):
    pos0 = lax.rem(pl.program_id(0) * tm, seq)
    u = xc_ref[...].astype(F32) * xi_ref[...].astype(F32)
    halo_p = xcp_ref[...].astype(F32) * xip_ref[...].astype(F32)
    halo_n = xcn_ref[...].astype(F32) * xin_ref[...].astype(F32)
    u_before = jnp.where(pos0 != 0, halo_p[BF16_SUBLANES - 1:BF16_SUBLANES, :], 0.0)
    u_after = jnp.where(pos0 + tm != seq, halo_n[0:1, :], 0.0)
    row = lax.broadcasted_iota(jnp.int32, (tm, 1), 0)
    u_m1 = jnp.where(row == 0, u_before, pltpu.roll(u, 1, 0))
    u_p1 = jnp.where(row == tm - 1, u_after, pltpu.roll(u, tm - 1, 0))
    cw = cw_ref[...]
    conv = cw[0:1, :] * u_m1 + cw[1:2, :] * u + cw[2:3, :] * u_p1
    a = (xb_ref[...].astype(F32) * conv).astype(BF16)
    mixed = jnp.concatenate([att_ref[...], a], axis=1)
    o_ref[...] = x_ref[...] + jnp.dot(mixed, wo_ref[...], preferred_element_type=F32)


def _outproj(x, u, cw, att, wo, seq, tm):
    t, d = x.shape
    rb = tm // BF16_SUBLANES
    last = t // BF16_SUBLANES - 1
    prev_map = lambda c: (lambda i: (jnp.maximum(i * rb - 1, 0), c))
    next_map = lambda c: (lambda i: (jnp.minimum((i + 1) * rb, last), c))
    return pl.pallas_call(
        functools.partial(_outproj_kernel, tm=tm, seq=seq),
        out_shape=jax.ShapeDtypeStruct((t, d), F32),
        grid=(t // tm,),
        in_specs=[
            pl.BlockSpec((tm, d), lambda i: (i, 0)),
            pl.BlockSpec((tm, CONV_DIM), lambda i: (i, 0)),
            pl.BlockSpec((tm, CONV_DIM), lambda i: (i, 1)),
            pl.BlockSpec((tm, CONV_DIM), lambda i: (i, 2)),
            pl.BlockSpec((BF16_SUBLANES, CONV_DIM), prev_map(1)),
            pl.BlockSpec((BF16_SUBLANES, CONV_DIM), prev_map(2)),
            pl.BlockSpec((BF16_SUBLANES, CONV_DIM), next_map(1)),
            pl.BlockSpec((BF16_SUBLANES, CONV_DIM), next_map(2)),
            pl.BlockSpec(cw.shape, lambda i: (0, 0)),
            pl.BlockSpec((tm, MLA_HEADS * V_DIM), lambda i: (i, 0)),
            pl.BlockSpec(wo.shape, lambda i: (0, 0)),
        ],
        out_specs=pl.BlockSpec((tm, d), lambda i: (i, 0)),
        compiler_params=_params("parallel"),
        name="outproj",
    )(x, u, u, u, u, u, u, u, cw, att, wo)


def _pool_kernel(x_ref, xp_ref, xn_ref, g_ref, wp_ref, ps_ref, o_ref, *, tm, seq):
    pos0 = lax.rem(pl.program_id(0) * tm, seq)
    x = x_ref[...]
    g = g_ref[...]
    xp = jnp.where(pos0 != 0, xp_ref[...], 0.0)
    xn = jnp.where(pos0 + tm != seq, xn_ref[...], 0.0)
    hext = _rms(jnp.concatenate([xp, x, xn], axis=0), g)
    n = tm + 2 * POOL_HALO
    pos = pos0 + lax.broadcasted_iota(jnp.int32, (tm, 1), 0)
    ps = ps_ref[...]
    for gi, w in enumerate(POOL_WINDOWS):
        c0, c1 = gi * POOL_GROUP, (gi + 1) * POOL_GROUP
        hg = hext[:, c0:c1]
        half = w // 2
        run = hg
        step = 1
        while step < half:
            run = run + pltpu.roll(run, n - step, 0)
            step *= 2
        win = run + pltpu.roll(run, half, 0)
        win = win[POOL_HALO:POOL_HALO + tm, :]
        cnt = jnp.minimum(pos + half, seq) - jnp.maximum(pos - half, 0)
        mean = win * (1.0 / cnt.astype(F32))
        p = (mean - hg[POOL_HALO:POOL_HALO + tm, :]).astype(BF16)
        y = jnp.dot(p, wp_ref[gi], preferred_element_type=F32)
        o_ref[:, c0:c1] = x[:, c0:c1] + y * ps[:, c0:c1]


def _pool(x, g, wp, ps, seq, tm):
    t, d = x.shape
    rb = tm // POOL_HALO
    last = t // POOL_HALO - 1
    return pl.pallas_call(
        functools.partial(_pool_kernel, tm=tm, seq=seq),
        out_shape=jax.ShapeDtypeStruct((t, d), F32),
        grid=(t // tm,),
        in_specs=[
            pl.BlockSpec((tm, d), lambda i: (i, 0)),
            pl.BlockSpec((POOL_HALO, d), lambda i: (jnp.maximum(i * rb - 1, 0), 0)),
            pl.BlockSpec((POOL_HALO, d), lambda i: (jnp.minimum((i + 1) * rb, last), 0)),
            pl.BlockSpec((1, d), lambda i: (0, 0)),
            pl.BlockSpec(wp.shape, lambda i: (0, 0, 0)),
            pl.BlockSpec((1, d), lambda i: (0, 0)),
        ],
        out_specs=pl.BlockSpec((tm, d), lambda i: (i, 0)),
        compiler_params=_params("parallel"),
        name="pool",
    )(x, x, x, g, wp, ps)


def _mlp_kernel(x_ref, g_ref, wu_ref, wd_ref, o_ref, h_ref):
    def down(h):
        a = jnp.dot(h, wu_ref[...], preferred_element_type=F32)
        a = jnp.square(jnp.maximum(a, 0.0)).astype(BF16)
        return jnp.dot(a, wd_ref[...], preferred_element_type=F32)

    @pl.when(pl.program_id(1) == 0)
    def _():
        x = x_ref[...]
        h = _rms(x, g_ref[...]).astype(BF16)
        h_ref[...] = h
        o_ref[...] = x + down(h)

    @pl.when(pl.program_id(1) != 0)
    def _():
        o_ref[...] += down(h_ref[...])


def _mlp(x, g, wu, wd, layer, tm, tf):
    t, d = x.shape
    f = wu.shape[2]
    return pl.pallas_call(
        _mlp_kernel,
        out_shape=jax.ShapeDtypeStruct((t, d), F32),
        grid=(t // tm, f // tf),
        in_specs=[
            pl.BlockSpec((tm, d), lambda i, j: (i, 0)),
            pl.BlockSpec((None, 1, d), lambda i, j: (layer, 0, 0)),
            pl.BlockSpec((None, d, tf), lambda i, j: (layer, 0, j)),
            pl.BlockSpec((None, tf, d), lambda i, j: (layer, j, 0)),
        ],
        out_specs=pl.BlockSpec((tm, d), lambda i, j: (i, 0)),
        scratch_shapes=[pltpu.VMEM((tm, d), BF16)],
        compiler_params=_params("parallel", "arbitrary", vmem=MLP_VMEM_LIMIT),
        name="mlp",
    )(x, g, wu, wd)


def _rope_tables(seq):
    inv = 1.0 / (ROPE_THETA ** (jnp.arange(0, ROPE_DIM, 2, dtype=F32) / ROPE_DIM))
    ang = jnp.arange(seq, dtype=F32)[:, None] * inv[None, :]
    cos, sin = jnp.cos(ang), jnp.sin(ang)
    reps = LANES // (ROPE_DIM // 2)
    sign = jnp.concatenate([-jnp.ones((LANES // 2,), F32), jnp.ones((LANES // 2,), F32)])
    return jnp.tile(cos, (1, reps)), jnp.tile(sin, (1, reps)) * sign[None, :]


def _pair_lanes(g):
    return jnp.repeat(g.reshape(2, ROPE_DIM // 2), LANES // ROPE_DIM, axis=0).reshape(1, LANES)


def _prepare_even(norm_mix0, w_in0, conv_w, q_a_norm, w_qb, kv_a_norm, w_kvb, q_norm, k_norm, w_o0):
    half = ROPE_DIM // 2
    base = IN0_DIM - ROPE_DIM
    w16 = w_in0.astype(BF16)
    x1, x2 = w16[:, base:base + half], w16[:, base + half:]
    pad = jnp.zeros((D_MODEL, IN0_PAD - base - LANES), BF16)
    w_in = jnp.concatenate([w16[:, :base], x1, x1, x2, x2, pad], axis=1)
    wq3 = w_qb.astype(BF16).reshape(Q_RANK, MLA_HEADS, QK_DIM)
    wq_rope = wq3[:, :, NOPE_DIM:].reshape(Q_RANK, MLA_HEADS // 2, 2, 2, half)
    wq_rope = wq_rope.transpose(0, 1, 3, 2, 4).reshape(Q_RANK, MLA_HEADS * ROPE_DIM)
    wqb = jnp.concatenate([wq3[:, :, :NOPE_DIM].reshape(Q_RANK, MLA_HEADS * NOPE_DIM), wq_rope], axis=1)
    scale = QK_DIM ** -0.5
    return dict(
        g0=norm_mix0.reshape(1, D_MODEL), w_in=w_in, cw=conv_w,
        gqa=q_a_norm.reshape(1, Q_RANK), gkva=kv_a_norm.reshape(1, KV_RANK),
        wqb=wqb, wkvb=w_kvb.astype(BF16),
        gqn=(q_norm[:NOPE_DIM] * scale).reshape(1, NOPE_DIM),
        gqr=_pair_lanes(q_norm[NOPE_DIM:] * scale),
        gkn=k_norm[:NOPE_DIM].reshape(1, NOPE_DIM),
        gkr=_pair_lanes(k_norm[NOPE_DIM:]),
        wo=jnp.concatenate([w_o0[CONV_DIM:], w_o0[:CONV_DIM]], axis=0).astype(BF16),
    )


def _tile(n, pref):
    return pref if n % pref == 0 else n


def _trunk(x3, even, odd, mlp, cos, sin):
    batch, seq, d = x3.shape
    t = batch * seq
    x = x3.reshape(t, d)
    tm_mlp = _tile(seq, 1024)
    tm = _tile(seq, 512)
    tq = seq if seq <= 2048 else 1024
    gm, wu, wd = mlp
    for i in range(gm.shape[0]):
        j = i // 2
        if i % 2 == 0:
            p = even[j]
            u = _inproj(x, p["g0"], p["w_in"], tm_mlp, 2048)
            qn, qr, k, v = _prep(u, cos, sin, p["gqa"], p["gkva"], p["wqb"], p["wkvb"],
                                 p["gqn"], p["gqr"], p["gkn"], p["gkr"], seq, tm)
            cast = ()
            if wu.dtype != BF16 and _flash_cast_rows((wu, wd), batch * MLA_HEADS * (seq // tq)):
                cast = (wu, wd)
            att, *casted = _flash(qn, qr, k, v, batch, seq, tq, 256, _tile(seq, 2048), cast)
            if casted:
                wu, wd = casted
            x = _outproj(x, u, p["cw"], att, p["wo"], seq, tm)
        else:
            g1, wp, ps = odd[j]
            x = _pool(x, g1, wp, ps, seq, tm)
        if wu.dtype != BF16:
            wu, wd = wu.astype(BF16), wd.astype(BF16)
        x = _mlp(x, gm, wu, wd, i, tm_mlp, 1024)
    return x.reshape(batch, seq, d), (gm, wu, wd)


def kernel(x_prompt, x_sample, norm_mix0, w_in0, conv_w, q_a_norm, w_qb, kv_a_norm, w_kvb, q_norm, k_norm, w_o0,
           norm_mix1, w_pool, pool_scale, norm_mlp, w_up, w_down):
    depth = norm_mlp.shape[0]
    even = [_prepare_even(norm_mix0[j], w_in0[j], conv_w[j], q_a_norm[j], w_qb[j], kv_a_norm[j], w_kvb[j],
                          q_norm[j], k_norm[j], w_o0[j]) for j in range(norm_mix0.shape[0])]
    odd = [(norm_mix1[j].reshape(1, D_MODEL), w_pool[j].astype(BF16), pool_scale[j].reshape(1, D_MODEL))
           for j in range(norm_mix1.shape[0])]
    mlp = (norm_mlp.reshape(depth, 1, D_MODEL), w_up, w_down)
    outs = []
    for x3 in (x_prompt, x_sample):
        cos, sin = _rope_tables(x3.shape[1])
        y, mlp = _trunk(x3, even, odd, mlp, cos, sin)
        outs.append(y)
    return tuple(outs)
```

```python
import functools

import jax
import jax.numpy as jnp
from jax import lax
from jax.experimental import pallas as pl
from jax.experimental.pallas import tpu as pltpu

F32 = jnp.float32
BF16 = jnp.bfloat16

D_MODEL = 2048
CONV_DIM = D_MODEL // 2
MLA_HEADS = 8
NOPE_DIM = 128
ROPE_DIM = 64
V_DIM = 128
QK_DIM = NOPE_DIM + ROPE_DIM
Q_RANK = D_MODEL // 4
KV_RANK = D_MODEL // 8
IN0_DIM = 3 * CONV_DIM + Q_RANK + KV_RANK + ROPE_DIM
IN0_PAD = 4096
POOL_WINDOWS = (2, 4, 8, 16)
POOL_GROUP = D_MODEL // len(POOL_WINDOWS)
POOL_HALO = 8
D_FF = 4 * D_MODEL
ROPE_THETA = 10000.0
EPS = 1e-6

LANES = 128
BF16_SUBLANES = 16
VMEM_LIMIT = 56 * 1024 * 1024
MLP_VMEM_LIMIT = 58 * 1024 * 1024


def _params(*sem, vmem=VMEM_LIMIT):
    return pltpu.CompilerParams(dimension_semantics=sem, vmem_limit_bytes=vmem)


def _rms(x, g):
    ms = jnp.mean(x * x, axis=-1, keepdims=True)
    return x * lax.rsqrt(ms + EPS) * g


def _inproj_kernel(x_ref, g_ref, w_ref, o_ref, h_ref):
    def project(h):
        return jnp.dot(h, w_ref[...], preferred_element_type=F32).astype(o_ref.dtype)

    @pl.when(pl.program_id(1) == 0)
    def _():
        h = _rms(x_ref[...], g_ref[...]).astype(BF16)
        h_ref[...] = h
        o_ref[...] = project(h)

    @pl.when(pl.program_id(1) != 0)
    def _():
        o_ref[...] = project(h_ref[...])


def _inproj(x, g, w, tm, tn):
    t, d = x.shape
    n = w.shape[1]
    return pl.pallas_call(
        _inproj_kernel,
        out_shape=jax.ShapeDtypeStruct((t, n), BF16),
        grid=(t // tm, n // tn),
        in_specs=[
            pl.BlockSpec((tm, d), lambda i, j: (i, 0)),
            pl.BlockSpec((1, d), lambda i, j: (0, 0)),
            pl.BlockSpec((d, tn), lambda i, j: (0, j)),
        ],
        out_specs=pl.BlockSpec((tm, tn), lambda i, j: (i, j)),
        scratch_shapes=[pltpu.VMEM((tm, d), BF16)],
        compiler_params=_params("parallel", "arbitrary"),
        name="inproj",
    )(x, g, w)


def _rope(r, cos, sin_signed):
    return r * cos + pltpu.roll(r, LANES // 2, 1) * sin_signed


def _prep_kernel(cq_ref, ckv_ref, kr_ref, cos_ref, sin_ref, gqa_ref, gkva_ref, wqb_ref, wkvb_ref,
                 gqn_ref, gqr_ref, gkn_ref, gkr_ref, qn_ref, qr_ref, k_ref, v_ref):
    cqn = _rms(cq_ref[...].astype(F32), gqa_ref[...]).astype(BF16)
    q = jnp.dot(cqn, wqb_ref[...], preferred_element_type=F32)
    ckvn = _rms(ckv_ref[...].astype(F32), gkva_ref[...]).astype(BF16)
    kv = jnp.dot(ckvn, wkvb_ref[...], preferred_element_type=F32)
    kr2 = kr_ref[...].astype(F32)

    lane = lax.broadcasted_iota(jnp.int32, (1, LANES), 1)
    lo = (lane % ROPE_DIM) < (ROPE_DIM // 2)
    cos = cos_ref[...]
    sin = sin_ref[...]
    gqn = gqn_ref[...]
    gqr = gqr_ref[...]
    gkn = gkn_ref[...]
    gkr = gkr_ref[...]
    inv_d = 1.0 / QK_DIM
    nope_all = MLA_HEADS * NOPE_DIM

    for j in range(MLA_HEADS // 2):
        rp = q[:, nope_all + j * LANES: nope_all + (j + 1) * LANES]
        rp2 = rp * rp
        rp2_pair = (jnp.where(lo, rp2, 0.0), jnp.where(lo, 0.0, rp2))
        r_pair = []
        for e in range(2):
            h = 2 * j + e
            nh = q[:, h * NOPE_DIM:(h + 1) * NOPE_DIM]
            ss = jnp.sum(nh * nh + rp2_pair[e], axis=-1, keepdims=True)
            r_h = lax.rsqrt(ss * inv_d + EPS)
            qn_ref[:, h * NOPE_DIM:(h + 1) * NOPE_DIM] = (nh * r_h * gqn).astype(BF16)
            r_pair.append(r_h)
        rscale = jnp.where(lo, r_pair[0], r_pair[1])
        qr_ref[:, j * LANES:(j + 1) * LANES] = _rope(rp * rscale * gqr, cos, sin).astype(BF16)

    kr_sq = jnp.where(lo, kr2 * kr2, 0.0)
    rope_k = _rope(kr2 * gkr, cos, sin)
    hw = NOPE_DIM + V_DIM
    for h in range(MLA_HEADS):
        kn = kv[:, h * hw: h * hw + NOPE_DIM]
        vh = kv[:, h * hw + NOPE_DIM:(h + 1) * hw]
        ss = jnp.sum(kn * kn + kr_sq, axis=-1, keepdims=True)
        r_h = lax.rsqrt(ss * inv_d + EPS)
        k_ref[:, h * hw: h * hw + NOPE_DIM] = (kn * r_h * gkn).astype(BF16)
        keep = lo if h % 2 == 0 else jnp.logical_not(lo)
        k_ref[:, h * hw + NOPE_DIM:(h + 1) * hw] = jnp.where(keep, rope_k * r_h, 0.0).astype(BF16)
        v_ref[:, h * V_DIM:(h + 1) * V_DIM] = vh.astype(BF16)


def _prep(u, cos, sin, gqa, gkva, wqb, wkvb, gqn, gqr, gkn, gkr, seq, tm):
    t = u.shape[0]
    nseq = seq // tm
    full = lambda a: pl.BlockSpec(a.shape, lambda i: (0,) * a.ndim)
    cq_blk = 3 * CONV_DIM // Q_RANK
    ckv_blk = (3 * CONV_DIM + Q_RANK) // KV_RANK
    kr_blk = (3 * CONV_DIM + Q_RANK + KV_RANK) // LANES
    return pl.pallas_call(
        _prep_kernel,
        out_shape=(
            jax.ShapeDtypeStruct((t, MLA_HEADS * NOPE_DIM), BF16),
            jax.ShapeDtypeStruct((t, MLA_HEADS * ROPE_DIM), BF16),
            jax.ShapeDtypeStruct((t, MLA_HEADS * (NOPE_DIM + V_DIM)), BF16),
            jax.ShapeDtypeStruct((t, MLA_HEADS * V_DIM), BF16),
        ),
        grid=(t // tm,),
        in_specs=[
            pl.BlockSpec((tm, Q_RANK), lambda i: (i, cq_blk)),
            pl.BlockSpec((tm, KV_RANK), lambda i: (i, ckv_blk)),
            pl.BlockSpec((tm, LANES), lambda i: (i, kr_blk)),
            pl.BlockSpec((tm, LANES), lambda i: (i % nseq, 0)),
            pl.BlockSpec((tm, LANES), lambda i: (i % nseq, 0)),
            full(gqa), full(gkva), full(wqb), full(wkvb), full(gqn), full(gqr), full(gkn), full(gkr),
        ],
        out_specs=(
            pl.BlockSpec((tm, MLA_HEADS * NOPE_DIM), lambda i: (i, 0)),
            pl.BlockSpec((tm, MLA_HEADS * ROPE_DIM), lambda i: (i, 0)),
            pl.BlockSpec((tm, MLA_HEADS * (NOPE_DIM + V_DIM)), lambda i: (i, 0)),
            pl.BlockSpec((tm, MLA_HEADS * V_DIM), lambda i: (i, 0)),
        ),
        compiler_params=_params("parallel"),
        name="mla_prep",
    )(u, u, u, cos, sin, gqa, gkva, wqb, wkvb, gqn, gqr, gkn, gkr)


def _flash_kernel(qn_ref, qr_ref, k_ref, v_ref, *rest, sub, ck, n_cast):
    src_refs, o_ref, dst_refs = rest[:n_cast], rest[n_cast], rest[n_cast + 1:]
    for src_ref, dst_ref in zip(src_refs, dst_refs):
        dst_ref[...] = src_ref[...].astype(dst_ref.dtype)
    tq = qn_ref.shape[0]
    seq = k_ref.shape[0]
    ones = jnp.ones((ck, LANES), BF16)
    vs = [jnp.concatenate([v_ref[c:c + ck, :], ones], axis=1) for c in range(0, seq, ck)]
    for r in range(0, tq, sub):
        q = jnp.concatenate([qn_ref[r:r + sub, :], qr_ref[r:r + sub, :]], axis=1)
        m = acc = None
        for ci, c in enumerate(range(0, seq, ck)):
            s = lax.dot_general(q, k_ref[c:c + ck, :], (((1,), (1,)), ((), ())), preferred_element_type=F32)
            mc = jnp.max(s, axis=-1, keepdims=True)
            m_new = mc if m is None else jnp.maximum(m, mc)
            p = jnp.exp(s - m_new)
            pv = jnp.dot(p.astype(BF16), vs[ci], preferred_element_type=F32)
            acc = pv if m is None else jnp.exp(m - m_new) * acc + pv
            m = m_new
        o_ref[r:r + sub, :] = (acc[:, :V_DIM] / acc[:, V_DIM:V_DIM + 1]).astype(o_ref.dtype)


CAST_MAX_BYTES = 2 * 1024 * 1024


def _flash_cast_rows(weights, steps):
    out = []
    for w in weights:
        layers, rows, cols = w.shape
        if steps % layers or rows % (steps // layers):
            return None
        r = rows // (steps // layers)
        if r % BF16_SUBLANES or r * cols * 4 > CAST_MAX_BYTES:
            return None
        out.append(r)
    return out


def _flash(qn, qr, k, v, batch, seq, tq, sub, ck, cast=()):
    t = qn.shape[0]
    nq = seq // tq
    steps = batch * MLA_HEADS * nq
    step_rows = _flash_cast_rows(cast, steps) if cast else []

    def slab(layers):
        per_layer = steps // layers

        def index(b, h, i):
            s = (b * MLA_HEADS + h) * nq + i
            return s // per_layer, s % per_layer, 0
        return index

    cast_specs = [pl.BlockSpec((None, r, w.shape[2]), slab(w.shape[0])) for r, w in zip(step_rows, cast)]
    return pl.pallas_call(
        functools.partial(_flash_kernel, sub=sub, ck=ck, n_cast=len(cast)),
        out_shape=[jax.ShapeDtypeStruct((t, MLA_HEADS * V_DIM), BF16)]
        + [jax.ShapeDtypeStruct(w.shape, BF16) for w in cast],
        grid=(batch, MLA_HEADS, nq),
        in_specs=[
            pl.BlockSpec((tq, NOPE_DIM), lambda b, h, i: (b * nq + i, h)),
            pl.BlockSpec((tq, LANES), lambda b, h, i: (b * nq + i, h // 2)),
            pl.BlockSpec((seq, NOPE_DIM + V_DIM), lambda b, h, i: (b, h)),
            pl.BlockSpec((seq, V_DIM), lambda b, h, i: (b, h)),
        ] + cast_specs,
        out_specs=[pl.BlockSpec((tq, V_DIM), lambda b, h, i: (b * nq + i, h))] + cast_specs,
        compiler_params=_params("parallel", "parallel", "arbitrary"),
        name="flash",
    )(qn, qr, k, v, *cast)


def _outproj_kernel(x_ref, xb_ref, xc_ref, xi_ref, xcp_ref, xip_ref, xcn_ref, xin_ref, cw_ref, att_ref,
                    wo_ref, o_ref, *, tm, seq):
    pos0 = lax.rem(pl.program_id(0) * tm, seq)
    u = xc_ref[...].astype(F32) * xi_ref[...].astype(F32)
    halo_p = xcp_ref[...].astype(F32) * xip_ref[...].astype(F32)
    halo_n = xcn_ref[...].astype(F32) * xin_ref[...].astype(F32)
    u_before = jnp.where(pos0 != 0, halo_p[BF16_SUBLANES - 1:BF16_SUBLANES, :], 0.0)
    u_after = jnp.where(pos0 + tm != seq, halo_n[0:1, :], 0.0)
    row = lax.broadcasted_iota(jnp.int32, (tm, 1), 0)
    u_m1 = jnp.where(row == 0, u_before, pltpu.roll(u, 1, 0))
    u_p1 = jnp.where(row == tm - 1, u_after, pltpu.roll(u, tm - 1, 0))
    cw = cw_ref[...]
    conv = cw[0:1, :] * u_m1 + cw[1:2, :] * u + cw[2:3, :] * u_p1
    a = (xb_ref[...].astype(F32) * conv).astype(BF16)
    mixed = jnp.concatenate([att_ref[...], a], axis=1)
    o_ref[...] = x_ref[...] + jnp.dot(mixed, wo_ref[...], preferred_element_type=F32)


def _outproj(x, u, cw, att, wo, seq, tm):
    t, d = x.shape
    rb = tm // BF16_SUBLANES
    last = t // BF16_SUBLANES - 1
    prev_map = lambda c: (lambda i: (jnp.maximum(i * rb - 1, 0), c))
    next_map = lambda c: (lambda i: (jnp.minimum((i + 1) * rb, last), c))
    return pl.pallas_call(
        functools.partial(_outproj_kernel, tm=tm, seq=seq),
        out_shape=jax.ShapeDtypeStruct((t, d), F32),
        grid=(t // tm,),
        in_specs=[
            pl.BlockSpec((tm, d), lambda i: (i, 0)),
            pl.BlockSpec((tm, CONV_DIM), lambda i: (i, 0)),
            pl.BlockSpec((tm, CONV_DIM), lambda i: (i, 1)),
            pl.BlockSpec((tm, CONV_DIM), lambda i: (i, 2)),
            pl.BlockSpec((BF16_SUBLANES, CONV_DIM), prev_map(1)),
            pl.BlockSpec((BF16_SUBLANES, CONV_DIM), prev_map(2)),
            pl.BlockSpec((BF16_SUBLANES, CONV_DIM), next_map(1)),
            pl.BlockSpec((BF16_SUBLANES, CONV_DIM), next_map(2)),
            pl.BlockSpec(cw.shape, lambda i: (0, 0)),
            pl.BlockSpec((tm, MLA_HEADS * V_DIM), lambda i: (i, 0)),
            pl.BlockSpec(wo.shape, lambda i: (0, 0)),
        ],
        out_specs=pl.BlockSpec((tm, d), lambda i: (i, 0)),
        compiler_params=_params("parallel"),
        name="outproj",
    )(x, u, u, u, u, u, u, u, cw, att, wo)


def _pool_kernel(x_ref, xp_ref, xn_ref, g_ref, wp_ref, ps_ref, o_ref, *, tm, seq):
    pos0 = lax.rem(pl.program_id(0) * tm, seq)
    x = x_ref[...]
    g = g_ref[...]
    xp = jnp.where(pos0 != 0, xp_ref[...], 0.0)
    xn = jnp.where(pos0 + tm != seq, xn_ref[...], 0.0)
    hext = _rms(jnp.concatenate([xp, x, xn], axis=0), g)
    n = tm + 2 * POOL_HALO
    pos = pos0 + lax.broadcasted_iota(jnp.int32, (tm, 1), 0)
    ps = ps_ref[...]
    for gi, w in enumerate(POOL_WINDOWS):
        c0, c1 = gi * POOL_GROUP, (gi + 1) * POOL_GROUP
        hg = hext[:, c0:c1]
        half = w // 2
        run = hg
        step = 1
        while step < half:
            run = run + pltpu.roll(run, n - step, 0)
            step *= 2
        win = run + pltpu.roll(run, half, 0)
        win = win[POOL_HALO:POOL_HALO + tm, :]
        cnt = jnp.minimum(pos + half, seq) - jnp.maximum(pos - half, 0)
        mean = win * (1.0 / cnt.astype(F32))
        p = (mean - hg[POOL_HALO:POOL_HALO + tm, :]).astype(BF16)
        y = jnp.dot(p, wp_ref[gi], preferred_element_type=F32)
        o_ref[:, c0:c1] = x[:, c0:c1] + y * ps[:, c0:c1]


def _pool(x, g, wp, ps, seq, tm):
    t, d = x.shape
    rb = tm // POOL_HALO
    last = t // POOL_HALO - 1
    return pl.pallas_call(
        functools.partial(_pool_kernel, tm=tm, seq=seq),
        out_shape=jax.ShapeDtypeStruct((t, d), F32),
        grid=(t // tm,),
        in_specs=[
            pl.BlockSpec((tm, d), lambda i: (i, 0)),
            pl.BlockSpec((POOL_HALO, d), lambda i: (jnp.maximum(i * rb - 1, 0), 0)),
            pl.BlockSpec((POOL_HALO, d), lambda i: (jnp.minimum((i + 1) * rb, last), 0)),
            pl.BlockSpec((1, d), lambda i: (0, 0)),
            pl.BlockSpec(wp.shape, lambda i: (0, 0, 0)),
            pl.BlockSpec((1, d), lambda i: (0, 0)),
        ],
        out_specs=pl.BlockSpec((tm, d), lambda i: (i, 0)),
        compiler_params=_params("parallel"),
        name="pool",
    )(x, x, x, g, wp, ps)


def _mlp_kernel(x_ref, g_ref, wu_ref, wd_ref, o_ref, h_ref):
    def down(h):
        a = jnp.dot(h, wu_ref[...], preferred_element_type=F32)
        a = jnp.square(jnp.maximum(a, 0.0)).astype(BF16)
        return jnp.dot(a, wd_ref[...], preferred_element_type=F32)

    @pl.when(pl.program_id(1) == 0)
    def _():
        x = x_ref[...]
        h = _rms(x, g_ref[...]).astype(BF16)
        h_ref[...] = h
        o_ref[...] = x + down(h)

    @pl.when(pl.program_id(1) != 0)
    def _():
        o_ref[...] += down(h_ref[...])


def _mlp(x, g, wu, wd, layer, tm, tf):
    t, d = x.shape
    f = wu.shape[2]
    return pl.pallas_call(
        _mlp_kernel,
        out_shape=jax.ShapeDtypeStruct((t, d), F32),
        grid=(t // tm, f // tf),
        in_specs=[
            pl.BlockSpec((tm, d), lambda i, j: (i, 0)),
            pl.BlockSpec((None, 1, d), lambda i, j: (layer, 0, 0)),
            pl.BlockSpec((None, d, tf), lambda i, j: (layer, 0, j)),
            pl.BlockSpec((None, tf, d), lambda i, j: (layer, j, 0)),
        ],
        out_specs=pl.BlockSpec((tm, d), lambda i, j: (i, 0)),
        scratch_shapes=[pltpu.VMEM((tm, d), BF16)],
        compiler_params=_params("parallel", "arbitrary", vmem=MLP_VMEM_LIMIT),
        name="mlp",
    )(x, g, wu, wd)


def _rope_tables(seq):
    inv = 1.0 / (ROPE_THETA ** (jnp.arange(0, ROPE_DIM, 2, dtype=F32) / ROPE_DIM))
    ang = jnp.arange(seq, dtype=F32)[:, None] * inv[None, :]
    cos, sin = jnp.cos(ang), jnp.sin(ang)
    reps = LANES // (ROPE_DIM // 2)
    sign = jnp.concatenate([-jnp.ones((LANES // 2,), F32), jnp.ones((LANES // 2,), F32)])
    return jnp.tile(cos, (1, reps)), jnp.tile(sin, (1, reps)) * sign[None, :]


def _pair_lanes(g):
    return jnp.repeat(g.reshape(2, ROPE_DIM // 2), LANES // ROPE_DIM, axis=0).reshape(1, LANES)


def _prepare_even(norm_mix0, w_in0, conv_w, q_a_norm, w_qb, kv_a_norm, w_kvb, q_norm, k_norm, w_o0):
    half = ROPE_DIM // 2
    base = IN0_DIM - ROPE_DIM
    w16 = w_in0.astype(BF16)
    x1, x2 = w16[:, base:base + half], w16[:, base + half:]
    pad = jnp.zeros((D_MODEL, IN0_PAD - base - LANES), BF16)
    w_in = jnp.concatenate([w16[:, :base], x1, x1, x2, x2, pad], axis=1)
    wq3 = w_qb.astype(BF16).reshape(Q_RANK, MLA_HEADS, QK_DIM)
    wq_rope = wq3[:, :, NOPE_DIM:].reshape(Q_RANK, MLA_HEADS // 2, 2, 2, half)
    wq_rope = wq_rope.transpose(0, 1, 3, 2, 4).reshape(Q_RANK, MLA_HEADS * ROPE_DIM)
    wqb = jnp.concatenate([wq3[:, :, :NOPE_DIM].reshape(Q_RANK, MLA_HEADS * NOPE_DIM), wq_rope], axis=1)
    scale = QK_DIM ** -0.5
    return dict(
        g0=norm_mix0.reshape(1, D_MODEL), w_in=w_in, cw=conv_w,
        gqa=q_a_norm.reshape(1, Q_RANK), gkva=kv_a_norm.reshape(1, KV_RANK),
        wqb=wqb, wkvb=w_kvb.astype(BF16),
        gqn=(q_norm[:NOPE_DIM] * scale).reshape(1, NOPE_DIM),
        gqr=_pair_lanes(q_norm[NOPE_DIM:] * scale),
        gkn=k_norm[:NOPE_DIM].reshape(1, NOPE_DIM),
        gkr=_pair_lanes(k_norm[NOPE_DIM:]),
        wo=jnp.concatenate([w_o0[CONV_DIM:], w_o0[:CONV_DIM]], axis=0).astype(BF16),
    )


def _tile(n, pref):
    return pref if n % pref == 0 else n


def _trunk(x3, even, odd, mlp, cos, sin):
    batch, seq, d = x3.shape
    t = batch * seq
    x = x3.reshape(t, d)
    tm_mlp = _tile(seq, 1024)
    tm = _tile(seq, 512)
    tq = seq if seq <= 2048 else 1024
    gm, wu, wd = mlp
    for i in range(gm.shape[0]):
        j = i // 2
        if i % 2 == 0:
            p = even[j]
            u = _inproj(x, p["g0"], p["w_in"], tm_mlp, 2048)
            qn, qr, k, v = _prep(u, cos, sin, p["gqa"], p["gkva"], p["wqb"], p["wkvb"],
                                 p["gqn"], p["gqr"], p["gkn"], p["gkr"], seq, tm)
            cast = ()
            if wu.dtype != BF16 and _flash_cast_rows((wu, wd), batch * MLA_HEADS * (seq // tq)):
                cast = (wu, wd)
            att, *casted = _flash(qn, qr, k, v, batch, seq, tq, 256, _tile(seq, 2048), cast)
            if casted:
                wu, wd = casted
            x = _outproj(x, u, p["cw"], att, p["wo"], seq, tm)
        else:
            g1, wp, ps = odd[j]
            x = _pool(x, g1, wp, ps, seq, tm)
        if wu.dtype != BF16:
            wu, wd = wu.astype(BF16), wd.astype(BF16)
        x = _mlp(x, gm, wu, wd, i, tm_mlp, 1024)
    return x.reshape(batch, seq, d), (gm, wu, wd)


def kernel(x_prompt, x_sample, norm_mix0, w_in0, conv_w, q_a_norm, w_qb, kv_a_norm, w_kvb, q_norm, k_norm, w_o0,
           norm_mix1, w_pool, pool_scale, norm_mlp, w_up, w_down):
    depth = norm_mlp.shape[0]
    even = [_prepare_even(norm_mix0[j], w_in0[j], conv_w[j], q_a_norm[j], w_qb[j], kv_a_norm[j], w_kvb[j],
                          q_norm[j], k_norm[j], w_o0[j]) for j in range(norm_mix0.shape[0])]
    odd = [(norm_mix1[j].reshape(1, D_MODEL), w_pool[j].astype(BF16), pool_scale[j].reshape(1, D_MODEL))
           for j in range(norm_mix1.shape[0])]
    mlp = (norm_mlp.reshape(depth, 1, D_MODEL), w_up, w_down)
    outs = []
    for x3 in (x_prompt, x_sample):
        cos, sin = _rope_tables(x3.shape[1])
        y, mlp = _trunk(x3, even, odd, mlp, cos, sin)
        outs.append(y)
    return tuple(outs)
```

```python
import functools

import jax
import jax.numpy as jnp
from jax import lax
from jax.experimental import pallas as pl
from jax.experimental.pallas import tpu as pltpu

F32 = jnp.float32
BF16 = jnp.bfloat16

D_MODEL = 2048
CONV_DIM = D_MODEL // 2
MLA_HEADS = 8
NOPE_DIM = 128
ROPE_DIM = 64
V_DIM = 128
QK_DIM = NOPE_DIM + ROPE_DIM
Q_RANK = D_MODEL // 4
KV_RANK = D_MODEL // 8
IN0_DIM = 3 * CONV_DIM + Q_RANK + KV_RANK + ROPE_DIM
IN0_PAD = 4096
POOL_WINDOWS = (2, 4, 8, 16)
POOL_GROUP = D_MODEL // len(POOL_WINDOWS)
POOL_HALO = 8
D_FF = 4 * D_MODEL
ROPE_THETA = 10000.0
EPS = 1e-6
LOG2_E = 1.4426950408889634

LANES = 128
BF16_SUBLANES = 16
VMEM_LIMIT = 56 * 1024 * 1024
MLP_VMEM_LIMIT = 58 * 1024 * 1024


def _params(*sem, vmem=VMEM_LIMIT):
    return pltpu.CompilerParams(dimension_semantics=sem, vmem_limit_bytes=vmem)


def _rms(x, g):
    ms = jnp.mean(x * x, axis=-1, keepdims=True)
    return x * lax.rsqrt(ms + EPS) * g


def _inproj_kernel(x_ref, g_ref, w_ref, o_ref, h_ref):
    def project(h):
        return jnp.dot(h, w_ref[...], preferred_element_type=F32).astype(o_ref.dtype)

    @pl.when(pl.program_id(1) == 0)
    def _():
        h = _rms(x_ref[...], g_ref[...]).astype(BF16)
        h_ref[...] = h
        o_ref[...] = project(h)

    @pl.when(pl.program_id(1) != 0)
    def _():
        o_ref[...] = project(h_ref[...])


def _inproj(x, g, w, tm, tn):
    t, d = x.shape
    n = w.shape[1]
    return pl.pallas_call(
        _inproj_kernel,
        out_shape=jax.ShapeDtypeStruct((t, n), BF16),
        grid=(t // tm, n // tn),
        in_specs=[
            pl.BlockSpec((tm, d), lambda i, j: (i, 0)),
            pl.BlockSpec((1, d), lambda i, j: (0, 0)),
            pl.BlockSpec((d, tn), lambda i, j: (0, j)),
        ],
        out_specs=pl.BlockSpec((tm, tn), lambda i, j: (i, j)),
        scratch_shapes=[pltpu.VMEM((tm, d), BF16)],
        compiler_params=_params("parallel", "arbitrary"),
        name="inproj",
    )(x, g, w)


def _rope(r, cos, sin_signed):
    return r * cos + pltpu.roll(r, LANES // 2, 1) * sin_signed


def _prep_kernel(cq_ref, ckv_ref, kr_ref, cos_ref, sin_ref, gqa_ref, gkva_ref, wqb_ref, wkvb_ref,
                 gqn_ref, gqr_ref, gkn_ref, gkr_ref, qn_ref, qr_ref, k_ref, v_ref):
    cqn = _rms(cq_ref[...].astype(F32), gqa_ref[...]).astype(BF16)
    q = jnp.dot(cqn, wqb_ref[...], preferred_element_type=F32)
    ckvn = _rms(ckv_ref[...].astype(F32), gkva_ref[...]).astype(BF16)
    kv = jnp.dot(ckvn, wkvb_ref[...], preferred_element_type=F32)
    kr2 = kr_ref[...].astype(F32)

    lane = lax.broadcasted_iota(jnp.int32, (1, LANES), 1)
    lo = (lane % ROPE_DIM) < (ROPE_DIM // 2)
    cos = cos_ref[...]
    sin = sin_ref[...]
    gqn = gqn_ref[...]
    gqr = gqr_ref[...]
    gkn = gkn_ref[...]
    gkr = gkr_ref[...]
    inv_d = 1.0 / QK_DIM
    nope_all = MLA_HEADS * NOPE_DIM

    for j in range(MLA_HEADS // 2):
        rp = q[:, nope_all + j * LANES: nope_all + (j + 1) * LANES]
        rp2 = rp * rp
        rp2_pair = (jnp.where(lo, rp2, 0.0), jnp.where(lo, 0.0, rp2))
        r_pair = []
        for e in range(2):
            h = 2 * j + e
            nh = q[:, h * NOPE_DIM:(h + 1) * NOPE_DIM]
            ss = jnp.sum(nh * nh + rp2_pair[e], axis=-1, keepdims=True)
            r_h = lax.rsqrt(ss * inv_d + EPS)
            qn_ref[:, h * NOPE_DIM:(h + 1) * NOPE_DIM] = (nh * r_h * gqn).astype(BF16)
            r_pair.append(r_h)
        rscale = jnp.where(lo, r_pair[0], r_pair[1])
        qr_ref[:, j * LANES:(j + 1) * LANES] = _rope(rp * rscale * gqr, cos, sin).astype(BF16)

    kr_sq = jnp.where(lo, kr2 * kr2, 0.0)
    rope_k = _rope(kr2 * gkr, cos, sin)
    hw = NOPE_DIM + V_DIM
    for h in range(MLA_HEADS):
        kn = kv[:, h * hw: h * hw + NOPE_DIM]
        vh = kv[:, h * hw + NOPE_DIM:(h + 1) * hw]
        ss = jnp.sum(kn * kn + kr_sq, axis=-1, keepdims=True)
        r_h = lax.rsqrt(ss * inv_d + EPS)
        k_ref[:, h * hw: h * hw + NOPE_DIM] = (kn * r_h * gkn).astype(BF16)
        keep = lo if h % 2 == 0 else jnp.logical_not(lo)
        k_ref[:, h * hw + NOPE_DIM:(h + 1) * hw] = jnp.where(keep, rope_k * r_h, 0.0).astype(BF16)
        v_ref[:, h * V_DIM:(h + 1) * V_DIM] = vh.astype(BF16)


def _prep(u, cos, sin, gqa, gkva, wqb, wkvb, gqn, gqr, gkn, gkr, seq, tm):
    t = u.shape[0]
    nseq = seq // tm
    full = lambda a: pl.BlockSpec(a.shape, lambda i: (0,) * a.ndim)
    cq_blk = 3 * CONV_DIM // Q_RANK
    ckv_blk = (3 * CONV_DIM + Q_RANK) // KV_RANK
    kr_blk = (3 * CONV_DIM + Q_RANK + KV_RANK) // LANES
    return pl.pallas_call(
        _prep_kernel,
        out_shape=(
            jax.ShapeDtypeStruct((t, MLA_HEADS * NOPE_DIM), BF16),
            jax.ShapeDtypeStruct((t, MLA_HEADS * ROPE_DIM), BF16),
            jax.ShapeDtypeStruct((t, MLA_HEADS * (NOPE_DIM + V_DIM)), BF16),
            jax.ShapeDtypeStruct((t, MLA_HEADS * V_DIM), BF16),
        ),
        grid=(t // tm,),
        in_specs=[
            pl.BlockSpec((tm, Q_RANK), lambda i: (i, cq_blk)),
            pl.BlockSpec((tm, KV_RANK), lambda i: (i, ckv_blk)),
            pl.BlockSpec((tm, LANES), lambda i: (i, kr_blk)),
            pl.BlockSpec((tm, LANES), lambda i: (i % nseq, 0)),
            pl.BlockSpec((tm, LANES), lambda i: (i % nseq, 0)),
            full(gqa), full(gkva), full(wqb), full(wkvb), full(gqn), full(gqr), full(gkn), full(gkr),
        ],
        out_specs=(
            pl.BlockSpec((tm, MLA_HEADS * NOPE_DIM), lambda i: (i, 0)),
            pl.BlockSpec((tm, MLA_HEADS * ROPE_DIM), lambda i: (i, 0)),
            pl.BlockSpec((tm, MLA_HEADS * (NOPE_DIM + V_DIM)), lambda i: (i, 0)),
            pl.BlockSpec((tm, MLA_HEADS * V_DIM), lambda i: (i, 0)),
        ),
        compiler_params=_params("parallel"),
        name="mla_prep",
    )(u, u, u, cos, sin, gqa, gkva, wqb, wkvb, gqn, gqr, gkn, gkr)


def _flash_kernel(qn_ref, qr_ref, k_ref, v_ref, *rest, sub, ck, n_cast):
    src_refs, o_ref, dst_refs = rest[:n_cast], rest[n_cast], rest[n_cast + 1:]
    for src_ref, dst_ref in zip(src_refs, dst_refs):
        dst_ref[...] = src_ref[...].astype(dst_ref.dtype)
    tq = qn_ref.shape[0]
    seq = k_ref.shape[0]
    ones = jnp.ones((ck, LANES), BF16)
    vs = [jnp.concatenate([v_ref[c:c + ck, :], ones], axis=1) for c in range(0, seq, ck)]
    for r in range(0, tq, sub):
        q = jnp.concatenate([qn_ref[r:r + sub, :], qr_ref[r:r + sub, :]], axis=1)
        m = acc = None
        for ci, c in enumerate(range(0, seq, ck)):
            s = lax.dot_general(q, k_ref[c:c + ck, :], (((1,), (1,)), ((), ())), preferred_element_type=F32)
            mc = jnp.max(s, axis=-1, keepdims=True)
            m_new = mc if m is None else jnp.maximum(m, mc)
            p = jnp.exp2(s - m_new)
            pv = jnp.dot(p.astype(BF16), vs[ci], preferred_element_type=F32)
            acc = pv if m is None else jnp.exp2(m - m_new) * acc + pv
            m = m_new
        o_ref[r:r + sub, :] = (acc[:, :V_DIM] / acc[:, V_DIM:V_DIM + 1]).astype(o_ref.dtype)


CAST_MAX_BYTES = 2 * 1024 * 1024


def _flash_cast_rows(weights, steps):
    out = []
    for w in weights:
        layers, rows, cols = w.shape
        if steps % layers or rows % (steps // layers):
            return None
        r = rows // (steps // layers)
        if r % BF16_SUBLANES or r * cols * 4 > CAST_MAX_BYTES:
            return None
        out.append(r)
    return out


def _flash(qn, qr, k, v, batch, seq, tq, sub, ck, cast=()):
    t = qn.shape[0]
    nq = seq // tq
    steps = batch * MLA_HEADS * nq
    step_rows = _flash_cast_rows(cast, steps) if cast else []

    def slab(layers):
        per_layer = steps // layers

        def index(b, h, i):
            s = (b * MLA_HEADS + h) * nq + i
            return s // per_layer, s % per_layer, 0
        return index

    cast_specs = [pl.BlockSpec((None, r, w.shape[2]), slab(w.shape[0])) for r, w in zip(step_rows, cast)]
    return pl.pallas_call(
        functools.partial(_flash_kernel, sub=sub, ck=ck, n_cast=len(cast)),
        out_shape=[jax.ShapeDtypeStruct((t, MLA_HEADS * V_DIM), BF16)]
        + [jax.ShapeDtypeStruct(w.shape, BF16) for w in cast],
        grid=(batch, MLA_HEADS, nq),
        in_specs=[
            pl.BlockSpec((tq, NOPE_DIM), lambda b, h, i: (b * nq + i, h)),
            pl.BlockSpec((tq, LANES), lambda b, h, i: (b * nq + i, h // 2)),
            pl.BlockSpec((seq, NOPE_DIM + V_DIM), lambda b, h, i: (b, h)),
            pl.BlockSpec((seq, V_DIM), lambda b, h, i: (b, h)),
        ] + cast_specs,
        out_specs=[pl.BlockSpec((tq, V_DIM), lambda b, h, i: (b * nq + i, h))] + cast_specs,
        compiler_params=_params("parallel", "parallel", "arbitrary"),
        name="flash",
    )(qn, qr, k, v, *cast)


def _outproj_kernel(x_ref, xb_ref, xc_ref, xi_ref, xcp_ref, xip_ref, xcn_ref, xin_ref, cw_ref, att_ref,
                    wo_ref, o_ref, *, tm, seq):
    pos0 = lax.rem(pl.program_id(0) * tm, seq)
    u = xc_ref[...].astype(F32) * xi_ref[...].astype(F32)
    halo_p = xcp_ref[...].astype(F32) * xip_ref[...].astype(F32)
    halo_n = xcn_ref[...].astype(F32) * xin_ref[...].astype(F32)
    u_before = jnp.where(pos0 != 0, halo_p[BF16_SUBLANES - 1:BF16_SUBLANES, :], 0.0)
    u_after = jnp.where(pos0 + tm != seq, halo_n[0:1, :], 0.0)
    row = lax.broadcasted_iota(jnp.int32, (tm, 1), 0)
    u_m1 = jnp.where(row == 0, u_before, pltpu.roll(u, 1, 0))
    u_p1 = jnp.where(row == tm - 1, u_after, pltpu.roll(u, tm - 1, 0))
    cw = cw_ref[...]
    conv = cw[0:1, :] * u_m1 + cw[1:2, :] * u + cw[2:3, :] * u_p1
    a = (xb_ref[...].astype(F32) * conv).astype(BF16)
    mixed = jnp.concatenate([att_ref[...], a], axis=1)
    o_ref[...] = x_ref[...] + jnp.dot(mixed, wo_ref[...], preferred_element_type=F32)


def _outproj(x, u, cw, att, wo, seq, tm):
    t, d = x.shape
    rb = tm // BF16_SUBLANES
    last = t // BF16_SUBLANES - 1
    prev_map = lambda c: (lambda i: (jnp.maximum(i * rb - 1, 0), c))
    next_map = lambda c: (lambda i: (jnp.minimum((i + 1) * rb, last), c))
    return pl.pallas_call(
        functools.partial(_outproj_kernel, tm=tm, seq=seq),
        out_shape=jax.ShapeDtypeStruct((t, d), F32),
        grid=(t // tm,),
        in_specs=[
            pl.BlockSpec((tm, d), lambda i: (i, 0)),
            pl.BlockSpec((tm, CONV_DIM), lambda i: (i, 0)),
            pl.BlockSpec((tm, CONV_DIM), lambda i: (i, 1)),
            pl.BlockSpec((tm, CONV_DIM), lambda i: (i, 2)),
            pl.BlockSpec((BF16_SUBLANES, CONV_DIM), prev_map(1)),
            pl.BlockSpec((BF16_SUBLANES, CONV_DIM), prev_map(2)),
            pl.BlockSpec((BF16_SUBLANES, CONV_DIM), next_map(1)),
            pl.BlockSpec((BF16_SUBLANES, CONV_DIM), next_map(2)),
            pl.BlockSpec(cw.shape, lambda i: (0, 0)),
            pl.BlockSpec((tm, MLA_HEADS * V_DIM), lambda i: (i, 0)),
            pl.BlockSpec(wo.shape, lambda i: (0, 0)),
        ],
        out_specs=pl.BlockSpec((tm, d), lambda i: (i, 0)),
        compiler_params=_params("parallel"),
        name="outproj",
    )(x, u, u, u, u, u, u, u, cw, att, wo)


def _pool_kernel(x_ref, xp_ref, xn_ref, g_ref, wp_ref, ps_ref, o_ref, *, tm, seq):
    pos0 = lax.rem(pl.program_id(0) * tm, seq)
    x = x_ref[...]
    g = g_ref[...]
    xp = jnp.where(pos0 != 0, xp_ref[...], 0.0)
    xn = jnp.where(pos0 + tm != seq, xn_ref[...], 0.0)
    hext = _rms(jnp.concatenate([xp, x, xn], axis=0), g)
    n = tm + 2 * POOL_HALO
    pos = pos0 + lax.broadcasted_iota(jnp.int32, (tm, 1), 0)
    ps = ps_ref[...]
    for gi, w in enumerate(POOL_WINDOWS):
        c0, c1 = gi * POOL_GROUP, (gi + 1) * POOL_GROUP
        hg = hext[:, c0:c1]
        half = w // 2
        run = hg
        step = 1
        while step < half:
            run = run + pltpu.roll(run, n - step, 0)
            step *= 2
        win = run + pltpu.roll(run, half, 0)
        win = win[POOL_HALO:POOL_HALO + tm, :]
        cnt = jnp.minimum(pos + half, seq) - jnp.maximum(pos - half, 0)
        mean = win * (1.0 / cnt.astype(F32))
        p = (mean - hg[POOL_HALO:POOL_HALO + tm, :]).astype(BF16)
        y = jnp.dot(p, wp_ref[gi], preferred_element_type=F32)
        o_ref[:, c0:c1] = x[:, c0:c1] + y * ps[:, c0:c1]


def _pool(x, g, wp, ps, seq, tm):
    t, d = x.shape
    rb = tm // POOL_HALO
    last = t // POOL_HALO - 1
    return pl.pallas_call(
        functools.partial(_pool_kernel, tm=tm, seq=seq),
        out_shape=jax.ShapeDtypeStruct((t, d), F32),
        grid=(t // tm,),
        in_specs=[
            pl.BlockSpec((tm, d), lambda i: (i, 0)),
            pl.BlockSpec((POOL_HALO, d), lambda i: (jnp.maximum(i * rb - 1, 0), 0)),
            pl.BlockSpec((POOL_HALO, d), lambda i: (jnp.minimum((i + 1) * rb, last), 0)),
            pl.BlockSpec((1, d), lambda i: (0, 0)),
            pl.BlockSpec(wp.shape, lambda i: (0, 0, 0)),
            pl.BlockSpec((1, d), lambda i: (0, 0)),
        ],
        out_specs=pl.BlockSpec((tm, d), lambda i: (i, 0)),
        compiler_params=_params("parallel"),
        name="pool",
    )(x, x, x, g, wp, ps)


def _mlp_kernel(x_ref, g_ref, wu_ref, wd_ref, o_ref, h_ref):
    def down(h):
        a = jnp.dot(h, wu_ref[...], preferred_element_type=F32)
        a = jnp.square(jnp.maximum(a, 0.0)).astype(BF16)
        return jnp.dot(a, wd_ref[...], preferred_element_type=F32)

    @pl.when(pl.program_id(1) == 0)
    def _():
        x = x_ref[...]
        h = _rms(x, g_ref[...]).astype(BF16)
        h_ref[...] = h
        o_ref[...] = x + down(h)

    @pl.when(pl.program_id(1) != 0)
    def _():
        o_ref[...] += down(h_ref[...])


def _mlp(x, g, wu, wd, layer, tm, tf):
    t, d = x.shape
    f = wu.shape[2]
    return pl.pallas_call(
        _mlp_kernel,
        out_shape=jax.ShapeDtypeStruct((t, d), F32),
        grid=(t // tm, f // tf),
        in_specs=[
            pl.BlockSpec((tm, d), lambda i, j: (i, 0)),
            pl.BlockSpec((None, 1, d), lambda i, j: (layer, 0, 0)),
            pl.BlockSpec((None, d, tf), lambda i, j: (layer, 0, j)),
            pl.BlockSpec((None, tf, d), lambda i, j: (layer, j, 0)),
        ],
        out_specs=pl.BlockSpec((tm, d), lambda i, j: (i, 0)),
        scratch_shapes=[pltpu.VMEM((tm, d), BF16)],
        compiler_params=_params("parallel", "arbitrary", vmem=MLP_VMEM_LIMIT),
        name="mlp",
    )(x, g, wu, wd)


def _rope_tables(seq):
    inv = 1.0 / (ROPE_THETA ** (jnp.arange(0, ROPE_DIM, 2, dtype=F32) / ROPE_DIM))
    ang = jnp.arange(seq, dtype=F32)[:, None] * inv[None, :]
    cos, sin = jnp.cos(ang), jnp.sin(ang)
    reps = LANES // (ROPE_DIM // 2)
    sign = jnp.concatenate([-jnp.ones((LANES // 2,), F32), jnp.ones((LANES // 2,), F32)])
    return jnp.tile(cos, (1, reps)), jnp.tile(sin, (1, reps)) * sign[None, :]


def _pair_lanes(g):
    return jnp.repeat(g.reshape(2, ROPE_DIM // 2), LANES // ROPE_DIM, axis=0).reshape(1, LANES)


def _prepare_even(norm_mix0, w_in0, conv_w, q_a_norm, w_qb, kv_a_norm, w_kvb, q_norm, k_norm, w_o0):
    half = ROPE_DIM // 2
    base = IN0_DIM - ROPE_DIM
    w16 = w_in0.astype(BF16)
    x1, x2 = w16[:, base:base + half], w16[:, base + half:]
    pad = jnp.zeros((D_MODEL, IN0_PAD - base - LANES), BF16)
    w_in = jnp.concatenate([w16[:, :base], x1, x1, x2, x2, pad], axis=1)
    wq3 = w_qb.astype(BF16).reshape(Q_RANK, MLA_HEADS, QK_DIM)
    wq_rope = wq3[:, :, NOPE_DIM:].reshape(Q_RANK, MLA_HEADS // 2, 2, 2, half)
    wq_rope = wq_rope.transpose(0, 1, 3, 2, 4).reshape(Q_RANK, MLA_HEADS * ROPE_DIM)
    wqb = jnp.concatenate([wq3[:, :, :NOPE_DIM].reshape(Q_RANK, MLA_HEADS * NOPE_DIM), wq_rope], axis=1)
    scale = QK_DIM ** -0.5 * LOG2_E
    return dict(
        g0=norm_mix0.reshape(1, D_MODEL), w_in=w_in, cw=conv_w,
        gqa=q_a_norm.reshape(1, Q_RANK), gkva=kv_a_norm.reshape(1, KV_RANK),
        wqb=wqb, wkvb=w_kvb.astype(BF16),
        gqn=(q_norm[:NOPE_DIM] * scale).reshape(1, NOPE_DIM),
        gqr=_pair_lanes(q_norm[NOPE_DIM:] * scale),
        gkn=k_norm[:NOPE_DIM].reshape(1, NOPE_DIM),
        gkr=_pair_lanes(k_norm[NOPE_DIM:]),
        wo=jnp.concatenate([w_o0[CONV_DIM:], w_o0[:CONV_DIM]], axis=0).astype(BF16),
    )


def _tile(n, pref):
    return pref if n % pref == 0 else n


def _trunk(x3, even, odd, mlp, cos, sin):
    batch, seq, d = x3.shape
    t = batch * seq
    x = x3.reshape(t, d)
    tm_mlp = _tile(seq, 1024)
    tm = _tile(seq, 512)
    tq = seq if seq <= 2048 else 1024
    gm, wu, wd = mlp
    for i in range(gm.shape[0]):
        j = i // 2
        if i % 2 == 0:
            p = even[j]
            u = _inproj(x, p["g0"], p["w_in"], tm_mlp, 2048)
            qn, qr, k, v = _prep(u, cos, sin, p["gqa"], p["gkva"], p["wqb"], p["wkvb"],
                                 p["gqn"], p["gqr"], p["gkn"], p["gkr"], seq, tm)
            cast = ()
            if wu.dtype != BF16 and _flash_cast_rows((wu, wd), batch * MLA_HEADS * (seq // tq)):
                cast = (wu, wd)
            att, *casted = _flash(qn, qr, k, v, batch, seq, tq, 256, _tile(seq, 2048), cast)
            if casted:
                wu, wd = casted
            x = _outproj(x, u, p["cw"], att, p["wo"], seq, tm)
        else:
            g1, wp, ps = odd[j]
            x = _pool(x, g1, wp, ps, seq, tm_mlp)
        if wu.dtype != BF16:
            wu, wd = wu.astype(BF16), wd.astype(BF16)
        x = _mlp(x, gm, wu, wd, i, tm_mlp, 1024)
    return x.reshape(batch, seq, d), (gm, wu, wd)


def kernel(x_prompt, x_sample, norm_mix0, w_in0, conv_w, q_a_norm, w_qb, kv_a_norm, w_kvb, q_norm, k_norm, w_o0,
           norm_mix1, w_pool, pool_scale, norm_mlp, w_up, w_down):
    depth = norm_mlp.shape[0]
    even = [_prepare_even(norm_mix0[j], w_in0[j], conv_w[j], q_a_norm[j], w_qb[j], kv_a_norm[j], w_kvb[j],
                          q_norm[j], k_norm[j], w_o0[j]) for j in range(norm_mix0.shape[0])]
    odd = [(norm_mix1[j].reshape(1, D_MODEL), w_pool[j].astype(BF16), pool_scale[j].reshape(1, D_MODEL))
           for j in range(norm_mix1.shape[0])]
    mlp = (norm_mlp.reshape(depth, 1, D_MODEL), w_up, w_down)
    outs = []
    for x3 in (x_prompt, x_sample):
        cos, sin = _rope_tables(x3.shape[1])
        y, mlp = _trunk(x3, even, odd, mlp, cos, sin)
        outs.append(y)
    return tuple(outs)
```

```python
import functools

import jax
import jax.numpy as jnp
from jax import lax
from jax.experimental import pallas as pl
from jax.experimental.pallas import tpu as pltpu

F32 = jnp.float32
BF16 = jnp.bfloat16

D_MODEL = 2048
CONV_DIM = D_MODEL // 2
MLA_HEADS = 8
NOPE_DIM = 128
ROPE_DIM = 64
V_DIM = 128
QK_DIM = NOPE_DIM + ROPE_DIM
Q_RANK = D_MODEL // 4
KV_RANK = D_MODEL // 8
IN0_DIM = 3 * CONV_DIM + Q_RANK + KV_RANK + ROPE_DIM
IN0_PAD = 4096
POOL_WINDOWS = (2, 4, 8, 16)
POOL_GROUP = D_MODEL // len(POOL_WINDOWS)
POOL_HALO = 8
ROPE_THETA = 10000.0
EPS = 1e-6
LOG2_E = 1.4426950408889634

LANES = 128
BF16_SUBLANES = 16
VMEM_LIMIT = 56 * 1024 * 1024
MLP_VMEM_LIMIT = 58 * 1024 * 1024

TOKENS_WIDE = 1024
TOKENS_NARROW = 512
INPROJ_COLS = 2048
MLP_HIDDEN = 1024
ATTN_Q_MAX = 2048
ATTN_Q_LONG = 1024
ATTN_Q_SUB = 256
ATTN_KEYS = 2048


def _params(*sem, vmem=VMEM_LIMIT):
    return pltpu.CompilerParams(dimension_semantics=sem, vmem_limit_bytes=vmem)


def _rms(x, g):
    ms = jnp.mean(x * x, axis=-1, keepdims=True)
    return x * lax.rsqrt(ms + EPS) * g


def _inproj_kernel(x_ref, g_ref, w_ref, o_ref, h_ref):
    def project(h):
        return jnp.dot(h, w_ref[...], preferred_element_type=F32).astype(o_ref.dtype)

    @pl.when(pl.program_id(1) == 0)
    def _():
        h = _rms(x_ref[...], g_ref[...]).astype(BF16)
        h_ref[...] = h
        o_ref[...] = project(h)

    @pl.when(pl.program_id(1) != 0)
    def _():
        o_ref[...] = project(h_ref[...])


def _inproj(x, g, w, tm, tn):
    t, d = x.shape
    n = w.shape[1]
    return pl.pallas_call(
        _inproj_kernel,
        out_shape=jax.ShapeDtypeStruct((t, n), BF16),
        grid=(t // tm, n // tn),
        in_specs=[
            pl.BlockSpec((tm, d), lambda i, j: (i, 0)),
            pl.BlockSpec((1, d), lambda i, j: (0, 0)),
            pl.BlockSpec((d, tn), lambda i, j: (0, j)),
        ],
        out_specs=pl.BlockSpec((tm, tn), lambda i, j: (i, j)),
        scratch_shapes=[pltpu.VMEM((tm, d), BF16)],
        compiler_params=_params("parallel", "arbitrary"),
        name="inproj",
    )(x, g, w)


def _rope(r, cos, sin_signed):
    return r * cos + pltpu.roll(r, LANES // 2, 1) * sin_signed


def _prep_kernel(cq_ref, ckv_ref, kr_ref, cos_ref, sin_ref, gqa_ref, gkva_ref, wqb_ref, wkvb_ref,
                 gqn_ref, gqr_ref, gkn_ref, gkr_ref, qn_ref, qr_ref, k_ref, v_ref):
    cqn = _rms(cq_ref[...].astype(F32), gqa_ref[...]).astype(BF16)
    q = jnp.dot(cqn, wqb_ref[...], preferred_element_type=F32)
    ckvn = _rms(ckv_ref[...].astype(F32), gkva_ref[...]).astype(BF16)
    kv = jnp.dot(ckvn, wkvb_ref[...], preferred_element_type=F32)
    kr2 = kr_ref[...].astype(F32)

    lane = lax.broadcasted_iota(jnp.int32, (1, LANES), 1)
    lo = (lane % ROPE_DIM) < (ROPE_DIM // 2)
    cos = cos_ref[...]
    sin = sin_ref[...]
    gqn = gqn_ref[...]
    gqr = gqr_ref[...]
    gkn = gkn_ref[...]
    gkr = gkr_ref[...]
    inv_d = 1.0 / QK_DIM
    nope_all = MLA_HEADS * NOPE_DIM

    for j in range(MLA_HEADS // 2):
        rp = q[:, nope_all + j * LANES: nope_all + (j + 1) * LANES]
        rp2 = rp * rp
        rp2_pair = (jnp.where(lo, rp2, 0.0), jnp.where(lo, 0.0, rp2))
        r_pair = []
        for e in range(2):
            h = 2 * j + e
            nh = q[:, h * NOPE_DIM:(h + 1) * NOPE_DIM]
            ss = jnp.sum(nh * nh + rp2_pair[e], axis=-1, keepdims=True)
            r_h = lax.rsqrt(ss * inv_d + EPS)
            qn_ref[:, h * NOPE_DIM:(h + 1) * NOPE_DIM] = (nh * r_h * gqn).astype(BF16)
            r_pair.append(r_h)
        rscale = jnp.where(lo, r_pair[0], r_pair[1])
        qr_ref[:, j * LANES:(j + 1) * LANES] = _rope(rp * rscale * gqr, cos, sin).astype(BF16)

    kr_sq = jnp.where(lo, kr2 * kr2, 0.0)
    rope_k = _rope(kr2 * gkr, cos, sin)
    hw = NOPE_DIM + V_DIM
    for h in range(MLA_HEADS):
        kn = kv[:, h * hw: h * hw + NOPE_DIM]
        vh = kv[:, h * hw + NOPE_DIM:(h + 1) * hw]
        ss = jnp.sum(kn * kn + kr_sq, axis=-1, keepdims=True)
        r_h = lax.rsqrt(ss * inv_d + EPS)
        k_ref[:, h * hw: h * hw + NOPE_DIM] = (kn * r_h * gkn).astype(BF16)
        keep = lo if h % 2 == 0 else jnp.logical_not(lo)
        k_ref[:, h * hw + NOPE_DIM:(h + 1) * hw] = jnp.where(keep, rope_k * r_h, 0.0).astype(BF16)
        v_ref[:, h * V_DIM:(h + 1) * V_DIM] = vh.astype(BF16)


def _prep(u, cos, sin, gqa, gkva, wqb, wkvb, gqn, gqr, gkn, gkr, seq, tm):
    t = u.shape[0]
    nseq = seq // tm
    full = lambda a: pl.BlockSpec(a.shape, lambda i: (0,) * a.ndim)
    cq_blk = 3 * CONV_DIM // Q_RANK
    ckv_blk = (3 * CONV_DIM + Q_RANK) // KV_RANK
    kr_blk = (3 * CONV_DIM + Q_RANK + KV_RANK) // LANES
    return pl.pallas_call(
        _prep_kernel,
        out_shape=(
            jax.ShapeDtypeStruct((t, MLA_HEADS * NOPE_DIM), BF16),
            jax.ShapeDtypeStruct((t, MLA_HEADS * ROPE_DIM), BF16),
            jax.ShapeDtypeStruct((t, MLA_HEADS * (NOPE_DIM + V_DIM)), BF16),
            jax.ShapeDtypeStruct((t, MLA_HEADS * V_DIM), BF16),
        ),
        grid=(t // tm,),
        in_specs=[
            pl.BlockSpec((tm, Q_RANK), lambda i: (i, cq_blk)),
            pl.BlockSpec((tm, KV_RANK), lambda i: (i, ckv_blk)),
            pl.BlockSpec((tm, LANES), lambda i: (i, kr_blk)),
            pl.BlockSpec((tm, LANES), lambda i: (i % nseq, 0)),
            pl.BlockSpec((tm, LANES), lambda i: (i % nseq, 0)),
            full(gqa), full(gkva), full(wqb), full(wkvb), full(gqn), full(gqr), full(gkn), full(gkr),
        ],
        out_specs=(
            pl.BlockSpec((tm, MLA_HEADS * NOPE_DIM), lambda i: (i, 0)),
            pl.BlockSpec((tm, MLA_HEADS * ROPE_DIM), lambda i: (i, 0)),
            pl.BlockSpec((tm, MLA_HEADS * (NOPE_DIM + V_DIM)), lambda i: (i, 0)),
            pl.BlockSpec((tm, MLA_HEADS * V_DIM), lambda i: (i, 0)),
        ),
        compiler_params=_params("parallel"),
        name="mla_prep",
    )(u, u, u, cos, sin, gqa, gkva, wqb, wkvb, gqn, gqr, gkn, gkr)


def _flash_kernel(qn_ref, qr_ref, k_ref, v_ref, *rest, sub, ck, n_cast):
    src_refs, o_ref, dst_refs = rest[:n_cast], rest[n_cast], rest[n_cast + 1:]
    for src_ref, dst_ref in zip(src_refs, dst_refs):
        dst_ref[...] = src_ref[...].astype(dst_ref.dtype)
    tq = qn_ref.shape[0]
    seq = k_ref.shape[0]
    ones = jnp.ones((ck, LANES), BF16)
    vs = [jnp.concatenate([v_ref[c:c + ck, :], ones], axis=1) for c in range(0, seq, ck)]
    for r in range(0, tq, sub):
        q = jnp.concatenate([qn_ref[r:r + sub, :], qr_ref[r:r + sub, :]], axis=1)
        m = acc = None
        for ci, c in enumerate(range(0, seq, ck)):
            s = lax.dot_general(q, k_ref[c:c + ck, :], (((1,), (1,)), ((), ())), preferred_element_type=F32)
            mc = jnp.max(s, axis=-1, keepdims=True)
            m_new = mc if m is None else jnp.maximum(m, mc)
            p = jnp.exp2(s - m_new)
            pv = jnp.dot(p.astype(BF16), vs[ci], preferred_element_type=F32)
            acc = pv if m is None else jnp.exp2(m - m_new) * acc + pv
            m = m_new
        o_ref[r:r + sub, :] = (acc[:, :V_DIM] / acc[:, V_DIM:V_DIM + 1]).astype(o_ref.dtype)


CAST_MAX_BYTES = 2 * 1024 * 1024


def _flash_cast_rows(weights, steps):
    out = []
    for w in weights:
        layers, rows, cols = w.shape
        if steps % layers or rows % (steps // layers):
            return None
        r = rows // (steps // layers)
        if r % BF16_SUBLANES or r * cols * 4 > CAST_MAX_BYTES:
            return None
        out.append(r)
    return out


def _flash(qn, qr, k, v, batch, seq, tq, sub, ck, cast=()):
    t = qn.shape[0]
    nq = seq // tq
    steps = batch * MLA_HEADS * nq
    step_rows = _flash_cast_rows(cast, steps) if cast else []

    def slab(layers):
        per_layer = steps // layers

        def index(b, h, i):
            s = (b * MLA_HEADS + h) * nq + i
            return s // per_layer, s % per_layer, 0
        return index

    cast_specs = [pl.BlockSpec((None, r, w.shape[2]), slab(w.shape[0])) for r, w in zip(step_rows, cast)]
    return pl.pallas_call(
        functools.partial(_flash_kernel, sub=sub, ck=ck, n_cast=len(cast)),
        out_shape=[jax.ShapeDtypeStruct((t, MLA_HEADS * V_DIM), BF16)]
        + [jax.ShapeDtypeStruct(w.shape, BF16) for w in cast],
        grid=(batch, MLA_HEADS, nq),
        in_specs=[
            pl.BlockSpec((tq, NOPE_DIM), lambda b, h, i: (b * nq + i, h)),
            pl.BlockSpec((tq, LANES), lambda b, h, i: (b * nq + i, h // 2)),
            pl.BlockSpec((seq, NOPE_DIM + V_DIM), lambda b, h, i: (b, h)),
            pl.BlockSpec((seq, V_DIM), lambda b, h, i: (b, h)),
        ] + cast_specs,
        out_specs=[pl.BlockSpec((tq, V_DIM), lambda b, h, i: (b * nq + i, h))] + cast_specs,
        compiler_params=_params("parallel", "parallel", "arbitrary"),
        name="flash",
    )(qn, qr, k, v, *cast)


def _outproj_kernel(x_ref, xb_ref, xc_ref, xi_ref, xcp_ref, xip_ref, xcn_ref, xin_ref, cw_ref, att_ref,
                    wo_ref, o_ref, *, tm, seq):
    pos0 = lax.rem(pl.program_id(0) * tm, seq)
    u = xc_ref[...].astype(F32) * xi_ref[...].astype(F32)
    halo_p = xcp_ref[...].astype(F32) * xip_ref[...].astype(F32)
    halo_n = xcn_ref[...].astype(F32) * xin_ref[...].astype(F32)
    u_before = jnp.where(pos0 != 0, halo_p[BF16_SUBLANES - 1:BF16_SUBLANES, :], 0.0)
    u_after = jnp.where(pos0 + tm != seq, halo_n[0:1, :], 0.0)
    row = lax.broadcasted_iota(jnp.int32, (tm, 1), 0)
    u_m1 = jnp.where(row == 0, u_before, pltpu.roll(u, 1, 0))
    u_p1 = jnp.where(row == tm - 1, u_after, pltpu.roll(u, tm - 1, 0))
    cw = cw_ref[...]
    conv = cw[0:1, :] * u_m1 + cw[1:2, :] * u + cw[2:3, :] * u_p1
    a = (xb_ref[...].astype(F32) * conv).astype(BF16)
    mixed = jnp.concatenate([att_ref[...], a], axis=1)
    o_ref[...] = x_ref[...] + jnp.dot(mixed, wo_ref[...], preferred_element_type=F32)


def _outproj(x, u, cw, att, wo, seq, tm):
    t, d = x.shape
    rb = tm // BF16_SUBLANES
    last = t // BF16_SUBLANES - 1
    prev_map = lambda c: (lambda i: (jnp.maximum(i * rb - 1, 0), c))
    next_map = lambda c: (lambda i: (jnp.minimum((i + 1) * rb, last), c))
    return pl.pallas_call(
        functools.partial(_outproj_kernel, tm=tm, seq=seq),
        out_shape=jax.ShapeDtypeStruct((t, d), F32),
        grid=(t // tm,),
        in_specs=[
            pl.BlockSpec((tm, d), lambda i: (i, 0)),
            pl.BlockSpec((tm, CONV_DIM), lambda i: (i, 0)),
            pl.BlockSpec((tm, CONV_DIM), lambda i: (i, 1)),
            pl.BlockSpec((tm, CONV_DIM), lambda i: (i, 2)),
            pl.BlockSpec((BF16_SUBLANES, CONV_DIM), prev_map(1)),
            pl.BlockSpec((BF16_SUBLANES, CONV_DIM), prev_map(2)),
            pl.BlockSpec((BF16_SUBLANES, CONV_DIM), next_map(1)),
            pl.BlockSpec((BF16_SUBLANES, CONV_DIM), next_map(2)),
            pl.BlockSpec(cw.shape, lambda i: (0, 0)),
            pl.BlockSpec((tm, MLA_HEADS * V_DIM), lambda i: (i, 0)),
            pl.BlockSpec(wo.shape, lambda i: (0, 0)),
        ],
        out_specs=pl.BlockSpec((tm, d), lambda i: (i, 0)),
        compiler_params=_params("parallel"),
        name="outproj",
    )(x, u, u, u, u, u, u, u, cw, att, wo)


def _pool_kernel(x_ref, xp_ref, xn_ref, g_ref, wp_ref, ps_ref, o_ref, *, tm, seq):
    pos0 = lax.rem(pl.program_id(0) * tm, seq)
    x = x_ref[...]
    g = g_ref[...]
    xp = jnp.where(pos0 != 0, xp_ref[...], 0.0)
    xn = jnp.where(pos0 + tm != seq, xn_ref[...], 0.0)
    hext = _rms(jnp.concatenate([xp, x, xn], axis=0), g)
    n = tm + 2 * POOL_HALO
    pos = pos0 + lax.broadcasted_iota(jnp.int32, (tm, 1), 0)
    ps = ps_ref[...]
    for gi, w in enumerate(POOL_WINDOWS):
        c0, c1 = gi * POOL_GROUP, (gi + 1) * POOL_GROUP
        hg = hext[:, c0:c1]
        half = w // 2
        run = hg
        step = 1
        while step < half:
            run = run + pltpu.roll(run, n - step, 0)
            step *= 2
        win = run + pltpu.roll(run, half, 0)
        win = win[POOL_HALO:POOL_HALO + tm, :]
        cnt = jnp.minimum(pos + half, seq) - jnp.maximum(pos - half, 0)
        mean = win * (1.0 / cnt.astype(F32))
        p = (mean - hg[POOL_HALO:POOL_HALO + tm, :]).astype(BF16)
        y = jnp.dot(p, wp_ref[gi], preferred_element_type=F32)
        o_ref[:, c0:c1] = x[:, c0:c1] + y * ps[:, c0:c1]


def _pool(x, g, wp, ps, seq, tm):
    t, d = x.shape
    rb = tm // POOL_HALO
    last = t // POOL_HALO - 1
    return pl.pallas_call(
        functools.partial(_pool_kernel, tm=tm, seq=seq),
        out_shape=jax.ShapeDtypeStruct((t, d), F32),
        grid=(t // tm,),
        in_specs=[
            pl.BlockSpec((tm, d), lambda i: (i, 0)),
            pl.BlockSpec((POOL_HALO, d), lambda i: (jnp.maximum(i * rb - 1, 0), 0)),
            pl.BlockSpec((POOL_HALO, d), lambda i: (jnp.minimum((i + 1) * rb, last), 0)),
            pl.BlockSpec((1, d), lambda i: (0, 0)),
            pl.BlockSpec(wp.shape, lambda i: (0, 0, 0)),
            pl.BlockSpec((1, d), lambda i: (0, 0)),
        ],
        out_specs=pl.BlockSpec((tm, d), lambda i: (i, 0)),
        compiler_params=_params("parallel"),
        name="pool",
    )(x, x, x, g, wp, ps)


def _mlp_kernel(x_ref, g_ref, wu_ref, wd_ref, o_ref, h_ref):
    def down(h):
        a = jnp.dot(h, wu_ref[...], preferred_element_type=F32)
        a = jnp.square(jnp.maximum(a, 0.0)).astype(BF16)
        return jnp.dot(a, wd_ref[...], preferred_element_type=F32)

    @pl.when(pl.program_id(1) == 0)
    def _():
        x = x_ref[...]
        h = _rms(x, g_ref[...]).astype(BF16)
        h_ref[...] = h
        o_ref[...] = x + down(h)

    @pl.when(pl.program_id(1) != 0)
    def _():
        o_ref[...] += down(h_ref[...])


def _mlp(x, g, wu, wd, layer, tm, tf):
    t, d = x.shape
    f = wu.shape[2]
    return pl.pallas_call(
        _mlp_kernel,
        out_shape=jax.ShapeDtypeStruct((t, d), F32),
        grid=(t // tm, f // tf),
        in_specs=[
            pl.BlockSpec((tm, d), lambda i, j: (i, 0)),
            pl.BlockSpec((None, 1, d), lambda i, j: (layer, 0, 0)),
            pl.BlockSpec((None, d, tf), lambda i, j: (layer, 0, j)),
            pl.BlockSpec((None, tf, d), lambda i, j: (layer, j, 0)),
        ],
        out_specs=pl.BlockSpec((tm, d), lambda i, j: (i, 0)),
        scratch_shapes=[pltpu.VMEM((tm, d), BF16)],
        compiler_params=_params("parallel", "arbitrary", vmem=MLP_VMEM_LIMIT),
        name="mlp",
    )(x, g, wu, wd)


def _rope_tables(seq):
    inv = 1.0 / (ROPE_THETA ** (jnp.arange(0, ROPE_DIM, 2, dtype=F32) / ROPE_DIM))
    ang = jnp.arange(seq, dtype=F32)[:, None] * inv[None, :]
    cos, sin = jnp.cos(ang), jnp.sin(ang)
    reps = LANES // (ROPE_DIM // 2)
    sign = jnp.concatenate([-jnp.ones((LANES // 2,), F32), jnp.ones((LANES // 2,), F32)])
    return jnp.tile(cos, (1, reps)), jnp.tile(sin, (1, reps)) * sign[None, :]


def _pair_lanes(g):
    return jnp.repeat(g.reshape(2, ROPE_DIM // 2), LANES // ROPE_DIM, axis=0).reshape(1, LANES)


def _prepare_even(norm_mix0, w_in0, conv_w, q_a_norm, w_qb, kv_a_norm, w_kvb, q_norm, k_norm, w_o0):
    half = ROPE_DIM // 2
    base = IN0_DIM - ROPE_DIM
    w16 = w_in0.astype(BF16)
    x1, x2 = w16[:, base:base + half], w16[:, base + half:]
    pad = jnp.zeros((D_MODEL, IN0_PAD - base - LANES), BF16)
    w_in = jnp.concatenate([w16[:, :base], x1, x1, x2, x2, pad], axis=1)
    wq3 = w_qb.astype(BF16).reshape(Q_RANK, MLA_HEADS, QK_DIM)
    wq_rope = wq3[:, :, NOPE_DIM:].reshape(Q_RANK, MLA_HEADS // 2, 2, 2, half)
    wq_rope = wq_rope.transpose(0, 1, 3, 2, 4).reshape(Q_RANK, MLA_HEADS * ROPE_DIM)
    wqb = jnp.concatenate([wq3[:, :, :NOPE_DIM].reshape(Q_RANK, MLA_HEADS * NOPE_DIM), wq_rope], axis=1)
    scale = QK_DIM ** -0.5 * LOG2_E
    return dict(
        g0=norm_mix0.reshape(1, D_MODEL), w_in=w_in, cw=conv_w,
        gqa=q_a_norm.reshape(1, Q_RANK), gkva=kv_a_norm.reshape(1, KV_RANK),
        wqb=wqb, wkvb=w_kvb.astype(BF16),
        gqn=(q_norm[:NOPE_DIM] * scale).reshape(1, NOPE_DIM),
        gqr=_pair_lanes(q_norm[NOPE_DIM:] * scale),
        gkn=k_norm[:NOPE_DIM].reshape(1, NOPE_DIM),
        gkr=_pair_lanes(k_norm[NOPE_DIM:]),
        wo=jnp.concatenate([w_o0[CONV_DIM:], w_o0[:CONV_DIM]], axis=0).astype(BF16),
    )


def _tile(n, pref):
    return pref if n % pref == 0 else n


def _trunk(x3, even, odd, mlp, cos, sin):
    batch, seq, d = x3.shape
    t = batch * seq
    x = x3.reshape(t, d)
    tm_mlp = _tile(seq, TOKENS_WIDE)
    tm = _tile(seq, TOKENS_NARROW)
    tq = seq if seq <= ATTN_Q_MAX else ATTN_Q_LONG
    gm, wu, wd = mlp
    for i in range(gm.shape[0]):
        j = i // 2
        if i % 2 == 0:
            p = even[j]
            u = _inproj(x, p["g0"], p["w_in"], tm_mlp, INPROJ_COLS)
            qn, qr, k, v = _prep(u, cos, sin, p["gqa"], p["gkva"], p["wqb"], p["wkvb"],
                                 p["gqn"], p["gqr"], p["gkn"], p["gkr"], seq, tm)
            cast = ()
            if wu.dtype != BF16 and _flash_cast_rows((wu, wd), batch * MLA_HEADS * (seq // tq)):
                cast = (wu, wd)
            att, *casted = _flash(qn, qr, k, v, batch, seq, tq, ATTN_Q_SUB, _tile(seq, ATTN_KEYS), cast)
            if casted:
                wu, wd = casted
            x = _outproj(x, u, p["cw"], att, p["wo"], seq, tm)
        else:
            g1, wp, ps = odd[j]
            x = _pool(x, g1, wp, ps, seq, tm_mlp)
        if wu.dtype != BF16:
            wu, wd = wu.astype(BF16), wd.astype(BF16)
        x = _mlp(x, gm, wu, wd, i, tm_mlp, MLP_HIDDEN)
    return x.reshape(batch, seq, d), (gm, wu, wd)


def kernel(x_prompt, x_sample, norm_mix0, w_in0, conv_w, q_a_norm, w_qb, kv_a_norm, w_kvb, q_norm, k_norm, w_o0,
           norm_mix1, w_pool, pool_scale, norm_mlp, w_up, w_down):
    depth = norm_mlp.shape[0]
    even = [_prepare_even(norm_mix0[j], w_in0[j], conv_w[j], q_a_norm[j], w_qb[j], kv_a_norm[j], w_kvb[j],
                          q_norm[j], k_norm[j], w_o0[j]) for j in range(norm_mix0.shape[0])]
    odd = [(norm_mix1[j].reshape(1, D_MODEL), w_pool[j].astype(BF16), pool_scale[j].reshape(1, D_MODEL))
           for j in range(norm_mix1.shape[0])]
    mlp = (norm_mlp.reshape(depth, 1, D_MODEL), w_up, w_down)
    outs = []
    for x3 in (x_prompt, x_sample):
        cos, sin = _rope_tables(x3.shape[1])
        y, mlp = _trunk(x3, even, odd, mlp, cos, sin)
        outs.append(y)
    return tuple(outs)
```

```python
import functools

import jax
import jax.numpy as jnp
from jax import lax
from jax.experimental import pallas as pl
from jax.experimental.pallas import tpu as pltpu

F32 = jnp.float32
BF16 = jnp.bfloat16

D_MODEL = 2048
CONV_DIM = D_MODEL // 2
MLA_HEADS = 8
NOPE_DIM = 128
ROPE_DIM = 64
V_DIM = 128
QK_DIM = NOPE_DIM + ROPE_DIM
Q_RANK = D_MODEL // 4
KV_RANK = D_MODEL // 8
IN0_DIM = 3 * CONV_DIM + Q_RANK + KV_RANK + ROPE_DIM
IN0_PAD = 4096
POOL_WINDOWS = (2, 4, 8, 16)
POOL_GROUP = D_MODEL // len(POOL_WINDOWS)
POOL_HALO = 8
ROPE_THETA = 10000.0
EPS = 1e-6
LOG2_E = 1.4426950408889634

LANES = 128
BF16_SUBLANES = 16
VMEM_LIMIT = 56 * 1024 * 1024
MLP_VMEM_LIMIT = 58 * 1024 * 1024

TOKENS_WIDE = 1024
TOKENS_NARROW = 512
INPROJ_COLS = 2048
MLP_HIDDEN = 1024
ATTN_Q_MAX = 2048
ATTN_Q_LONG = 1024
ATTN_Q_SUB = 256
ATTN_KEYS = 2048


def _params(*sem, vmem=VMEM_LIMIT):
    return pltpu.CompilerParams(dimension_semantics=sem, vmem_limit_bytes=vmem)


def _rms(x, g):
    ms = jnp.mean(x * x, axis=-1, keepdims=True)
    return x * lax.rsqrt(ms + EPS) * g


def _inproj_kernel(x_ref, g_ref, w_ref, o_ref, h_ref):
    def project(h):
        return jnp.dot(h, w_ref[...], preferred_element_type=F32).astype(o_ref.dtype)

    @pl.when(pl.program_id(1) == 0)
    def _():
        h = _rms(x_ref[...], g_ref[...]).astype(BF16)
        h_ref[...] = h
        o_ref[...] = project(h)

    @pl.when(pl.program_id(1) != 0)
    def _():
        o_ref[...] = project(h_ref[...])


def _inproj(x, g, w, tm, tn):
    t, d = x.shape
    n = w.shape[1]
    return pl.pallas_call(
        _inproj_kernel,
        out_shape=jax.ShapeDtypeStruct((t, n), BF16),
        grid=(t // tm, n // tn),
        in_specs=[
            pl.BlockSpec((tm, d), lambda i, j: (i, 0)),
            pl.BlockSpec((1, d), lambda i, j: (0, 0)),
            pl.BlockSpec((d, tn), lambda i, j: (0, j)),
        ],
        out_specs=pl.BlockSpec((tm, tn), lambda i, j: (i, j)),
        scratch_shapes=[pltpu.VMEM((tm, d), BF16)],
        compiler_params=_params("parallel", "arbitrary"),
        name="inproj",
    )(x, g, w)


def _rope(r, cos, sin_signed):
    return r * cos + pltpu.roll(r, LANES // 2, 1) * sin_signed


def _prep_kernel(cq_ref, ckv_ref, kr_ref, cos_ref, sin_ref, gqa_ref, gkva_ref, wqb_ref, wkvb_ref,
                 gqn_ref, gqr_ref, gkn_ref, gkr_ref, qn_ref, qr_ref, k_ref, v_ref):
    cqn = _rms(cq_ref[...].astype(F32), gqa_ref[...]).astype(BF16)
    q = jnp.dot(cqn, wqb_ref[...], preferred_element_type=F32)
    ckvn = _rms(ckv_ref[...].astype(F32), gkva_ref[...]).astype(BF16)
    kv = jnp.dot(ckvn, wkvb_ref[...], preferred_element_type=F32)
    kr2 = kr_ref[...].astype(F32)

    lane = lax.broadcasted_iota(jnp.int32, (1, LANES), 1)
    lo = (lane % ROPE_DIM) < (ROPE_DIM // 2)
    cos = cos_ref[...]
    sin = sin_ref[...]
    gqn = gqn_ref[...]
    gqr = gqr_ref[...]
    gkn = gkn_ref[...]
    gkr = gkr_ref[...]
    inv_d = 1.0 / QK_DIM
    nope_all = MLA_HEADS * NOPE_DIM

    for j in range(MLA_HEADS // 2):
        rp = q[:, nope_all + j * LANES: nope_all + (j + 1) * LANES]
        rp2 = rp * rp
        rp2_pair = (jnp.where(lo, rp2, 0.0), jnp.where(lo, 0.0, rp2))
        r_pair = []
        for e in range(2):
            h = 2 * j + e
            nh = q[:, h * NOPE_DIM:(h + 1) * NOPE_DIM]
            ss = jnp.sum(nh * nh + rp2_pair[e], axis=-1, keepdims=True)
            r_h = lax.rsqrt(ss * inv_d + EPS)
            qn_ref[:, h * NOPE_DIM:(h + 1) * NOPE_DIM] = (nh * r_h * gqn).astype(BF16)
            r_pair.append(r_h)
        rscale = jnp.where(lo, r_pair[0], r_pair[1])
        qr_ref[:, j * LANES:(j + 1) * LANES] = _rope(rp * rscale * gqr, cos, sin).astype(BF16)

    kr_sq = jnp.where(lo, kr2 * kr2, 0.0)
    rope_k = _rope(kr2 * gkr, cos, sin)
    hw = NOPE_DIM + V_DIM
    for h in range(MLA_HEADS):
        kn = kv[:, h * hw: h * hw + NOPE_DIM]
        vh = kv[:, h * hw + NOPE_DIM:(h + 1) * hw]
        ss = jnp.sum(kn * kn + kr_sq, axis=-1, keepdims=True)
        r_h = lax.rsqrt(ss * inv_d + EPS)
        k_ref[:, h * hw: h * hw + NOPE_DIM] = (kn * r_h * gkn).astype(BF16)
        keep = lo if h % 2 == 0 else jnp.logical_not(lo)
        k_ref[:, h * hw + NOPE_DIM:(h + 1) * hw] = jnp.where(keep, rope_k * r_h, 0.0).astype(BF16)
        v_ref[:, h * V_DIM:(h + 1) * V_DIM] = vh.astype(BF16)


def _prep(u, cos, sin, gqa, gkva, wqb, wkvb, gqn, gqr, gkn, gkr, seq, tm):
    t = u.shape[0]
    nseq = seq // tm
    full = lambda a: pl.BlockSpec(a.shape, lambda i: (0,) * a.ndim)
    cq_blk = 3 * CONV_DIM // Q_RANK
    ckv_blk = (3 * CONV_DIM + Q_RANK) // KV_RANK
    kr_blk = (3 * CONV_DIM + Q_RANK + KV_RANK) // LANES
    return pl.pallas_call(
        _prep_kernel,
        out_shape=(
            jax.ShapeDtypeStruct((t, MLA_HEADS * NOPE_DIM), BF16),
            jax.ShapeDtypeStruct((t, MLA_HEADS * ROPE_DIM), BF16),
            jax.ShapeDtypeStruct((t, MLA_HEADS * (NOPE_DIM + V_DIM)), BF16),
            jax.ShapeDtypeStruct((t, MLA_HEADS * V_DIM), BF16),
        ),
        grid=(t // tm,),
        in_specs=[
            pl.BlockSpec((tm, Q_RANK), lambda i: (i, cq_blk)),
            pl.BlockSpec((tm, KV_RANK), lambda i: (i, ckv_blk)),
            pl.BlockSpec((tm, LANES), lambda i: (i, kr_blk)),
            pl.BlockSpec((tm, LANES), lambda i: (i % nseq, 0)),
            pl.BlockSpec((tm, LANES), lambda i: (i % nseq, 0)),
            full(gqa), full(gkva), full(wqb), full(wkvb), full(gqn), full(gqr), full(gkn), full(gkr),
        ],
        out_specs=(
            pl.BlockSpec((tm, MLA_HEADS * NOPE_DIM), lambda i: (i, 0)),
            pl.BlockSpec((tm, MLA_HEADS * ROPE_DIM), lambda i: (i, 0)),
            pl.BlockSpec((tm, MLA_HEADS * (NOPE_DIM + V_DIM)), lambda i: (i, 0)),
            pl.BlockSpec((tm, MLA_HEADS * V_DIM), lambda i: (i, 0)),
        ),
        compiler_params=_params("parallel"),
        name="mla_prep",
    )(u, u, u, cos, sin, gqa, gkva, wqb, wkvb, gqn, gqr, gkn, gkr)


def _flash_kernel(qn_ref, qr_ref, k_ref, v_ref, *rest, sub, ck, n_cast):
    src_refs, o_ref, dst_refs = rest[:n_cast], rest[n_cast], rest[n_cast + 1:]
    for src_ref, dst_ref in zip(src_refs, dst_refs):
        dst_ref[...] = src_ref[...].astype(dst_ref.dtype)
    tq = qn_ref.shape[0]
    seq = k_ref.shape[0]
    ones = jnp.ones((ck, LANES), BF16)
    vs = [jnp.concatenate([v_ref[c:c + ck, :], ones], axis=1) for c in range(0, seq, ck)]
    for r in range(0, tq, sub):
        q = jnp.concatenate([qn_ref[r:r + sub, :], qr_ref[r:r + sub, :]], axis=1)
        m = acc = None
        for ci, c in enumerate(range(0, seq, ck)):
            s = lax.dot_general(q, k_ref[c:c + ck, :], (((1,), (1,)), ((), ())), preferred_element_type=F32)
            mc = jnp.max(s, axis=-1, keepdims=True)
            m_new = mc if m is None else jnp.maximum(m, mc)
            p = jnp.exp2(s - m_new)
            pv = jnp.dot(p.astype(BF16), vs[ci], preferred_element_type=F32)
            acc = pv if m is None else jnp.exp2(m - m_new) * acc + pv
            m = m_new
        o_ref[r:r + sub, :] = (acc[:, :V_DIM] / acc[:, V_DIM:V_DIM + 1]).astype(o_ref.dtype)


CAST_MAX_BYTES = 2 * 1024 * 1024


def _flash_cast_rows(weights, steps):
    out = []
    for w in weights:
        layers, rows, cols = w.shape
        if steps % layers or rows % (steps // layers):
            return None
        r = rows // (steps // layers)
        if r % BF16_SUBLANES or r * cols * 4 > CAST_MAX_BYTES:
            return None
        out.append(r)
    return out


def _flash(qn, qr, k, v, batch, seq, tq, sub, ck, cast=()):
    t = qn.shape[0]
    nq = seq // tq
    steps = batch * MLA_HEADS * nq
    step_rows = _flash_cast_rows(cast, steps) if cast else []

    def slab(layers):
        per_layer = steps // layers

        def index(b, h, i):
            s = (b * MLA_HEADS + h) * nq + i
            return s // per_layer, s % per_layer, 0
        return index

    cast_specs = [pl.BlockSpec((None, r, w.shape[2]), slab(w.shape[0])) for r, w in zip(step_rows, cast)]
    return pl.pallas_call(
        functools.partial(_flash_kernel, sub=sub, ck=ck, n_cast=len(cast)),
        out_shape=[jax.ShapeDtypeStruct((t, MLA_HEADS * V_DIM), BF16)]
        + [jax.ShapeDtypeStruct(w.shape, BF16) for w in cast],
        grid=(batch, MLA_HEADS, nq),
        in_specs=[
            pl.BlockSpec((tq, NOPE_DIM), lambda b, h, i: (b * nq + i, h)),
            pl.BlockSpec((tq, LANES), lambda b, h, i: (b * nq + i, h // 2)),
            pl.BlockSpec((seq, NOPE_DIM + V_DIM), lambda b, h, i: (b, h)),
            pl.BlockSpec((seq, V_DIM), lambda b, h, i: (b, h)),
        ] + cast_specs,
        out_specs=[pl.BlockSpec((tq, V_DIM), lambda b, h, i: (b * nq + i, h))] + cast_specs,
        compiler_params=_params("parallel", "parallel", "arbitrary"),
        name="flash",
    )(qn, qr, k, v, *cast)


def _outproj_kernel(x_ref, xb_ref, xc_ref, xi_ref, xcp_ref, xip_ref, xcn_ref, xin_ref, cw_ref, att_ref,
                    wo_ref, o_ref, *, tm, seq):
    pos0 = lax.rem(pl.program_id(0) * tm, seq)
    u = (xc_ref[...] * xi_ref[...]).astype(F32)
    halo_p = (xcp_ref[...] * xip_ref[...]).astype(F32)
    halo_n = (xcn_ref[...] * xin_ref[...]).astype(F32)
    half = BF16_SUBLANES // 2
    before = jnp.where(pos0 != 0, halo_p[half:, :], 0.0)
    after = jnp.where(pos0 + tm != seq, halo_n[:half, :], 0.0)
    uext = jnp.concatenate([before, u, after], axis=0)
    u_m1 = pltpu.roll(uext, 1, 0)[half:half + tm, :]
    u_p1 = pltpu.roll(uext, tm + BF16_SUBLANES - 1, 0)[half:half + tm, :]
    cw = cw_ref[...]
    conv = cw[0:1, :] * u_m1 + cw[1:2, :] * u + cw[2:3, :] * u_p1
    a = xb_ref[...] * conv.astype(BF16)
    mixed = jnp.concatenate([att_ref[...], a], axis=1)
    o_ref[...] = x_ref[...] + jnp.dot(mixed, wo_ref[...], preferred_element_type=F32)


def _outproj(x, u, cw, att, wo, seq, tm):
    t, d = x.shape
    rb = tm // BF16_SUBLANES
    last = t // BF16_SUBLANES - 1
    prev_map = lambda c: (lambda i: (jnp.maximum(i * rb - 1, 0), c))
    next_map = lambda c: (lambda i: (jnp.minimum((i + 1) * rb, last), c))
    return pl.pallas_call(
        functools.partial(_outproj_kernel, tm=tm, seq=seq),
        out_shape=jax.ShapeDtypeStruct((t, d), F32),
        grid=(t // tm,),
        in_specs=[
            pl.BlockSpec((tm, d), lambda i: (i, 0)),
            pl.BlockSpec((tm, CONV_DIM), lambda i: (i, 0)),
            pl.BlockSpec((tm, CONV_DIM), lambda i: (i, 1)),
            pl.BlockSpec((tm, CONV_DIM), lambda i: (i, 2)),
            pl.BlockSpec((BF16_SUBLANES, CONV_DIM), prev_map(1)),
            pl.BlockSpec((BF16_SUBLANES, CONV_DIM), prev_map(2)),
            pl.BlockSpec((BF16_SUBLANES, CONV_DIM), next_map(1)),
            pl.BlockSpec((BF16_SUBLANES, CONV_DIM), next_map(2)),
            pl.BlockSpec(cw.shape, lambda i: (0, 0)),
            pl.BlockSpec((tm, MLA_HEADS * V_DIM), lambda i: (i, 0)),
            pl.BlockSpec(wo.shape, lambda i: (0, 0)),
        ],
        out_specs=pl.BlockSpec((tm, d), lambda i: (i, 0)),
        compiler_params=_params("parallel"),
        name="outproj",
    )(x, u, u, u, u, u, u, u, cw, att, wo)


def _pool_kernel(x_ref, xp_ref, xn_ref, g_ref, wp_ref, ps_ref, o_ref, *, tm, seq):
    pos0 = lax.rem(pl.program_id(0) * tm, seq)
    x = x_ref[...]
    g = g_ref[...]
    xp = jnp.where(pos0 != 0, xp_ref[...], 0.0)
    xn = jnp.where(pos0 + tm != seq, xn_ref[...], 0.0)
    hext = _rms(jnp.concatenate([xp, x, xn], axis=0), g)
    n = tm + 2 * POOL_HALO
    pos = pos0 + lax.broadcasted_iota(jnp.int32, (tm, 1), 0)
    ps = ps_ref[...]
    for gi, w in enumerate(POOL_WINDOWS):
        c0, c1 = gi * POOL_GROUP, (gi + 1) * POOL_GROUP
        hg = hext[:, c0:c1]
        half = w // 2
        run = hg
        step = 1
        while step < half:
            run = run + pltpu.roll(run, n - step, 0)
            step *= 2
        win = run + pltpu.roll(run, half, 0)
        win = win[POOL_HALO:POOL_HALO + tm, :]
        cnt = jnp.minimum(pos + half, seq) - jnp.maximum(pos - half, 0)
        mean = win * (1.0 / cnt.astype(F32))
        p = (mean - hg[POOL_HALO:POOL_HALO + tm, :]).astype(BF16)
        y = jnp.dot(p, wp_ref[gi], preferred_element_type=F32)
        o_ref[:, c0:c1] = x[:, c0:c1] + y * ps[:, c0:c1]


def _pool(x, g, wp, ps, seq, tm):
    t, d = x.shape
    rb = tm // POOL_HALO
    last = t // POOL_HALO - 1
    return pl.pallas_call(
        functools.partial(_pool_kernel, tm=tm, seq=seq),
        out_shape=jax.ShapeDtypeStruct((t, d), F32),
        grid=(t // tm,),
        in_specs=[
            pl.BlockSpec((tm, d), lambda i: (i, 0)),
            pl.BlockSpec((POOL_HALO, d), lambda i: (jnp.maximum(i * rb - 1, 0), 0)),
            pl.BlockSpec((POOL_HALO, d), lambda i: (jnp.minimum((i + 1) * rb, last), 0)),
            pl.BlockSpec((1, d), lambda i: (0, 0)),
            pl.BlockSpec(wp.shape, lambda i: (0, 0, 0)),
            pl.BlockSpec((1, d), lambda i: (0, 0)),
        ],
        out_specs=pl.BlockSpec((tm, d), lambda i: (i, 0)),
        compiler_params=_params("parallel"),
        name="pool",
    )(x, x, x, g, wp, ps)


def _mlp_kernel(x_ref, g_ref, wu_ref, wd_ref, o_ref, h_ref):
    def down(h):
        a = jnp.dot(h, wu_ref[...], preferred_element_type=F32)
        a = jnp.square(jnp.maximum(a, 0.0)).astype(BF16)
        return jnp.dot(a, wd_ref[...], preferred_element_type=F32)

    @pl.when(pl.program_id(1) == 0)
    def _():
        x = x_ref[...]
        h = _rms(x, g_ref[...]).astype(BF16)
        h_ref[...] = h
        o_ref[...] = x + down(h)

    @pl.when(pl.program_id(1) != 0)
    def _():
        o_ref[...] += down(h_ref[...])


def _mlp(x, g, wu, wd, layer, tm, tf):
    t, d = x.shape
    f = wu.shape[2]
    return pl.pallas_call(
        _mlp_kernel,
        out_shape=jax.ShapeDtypeStruct((t, d), F32),
        grid=(t // tm, f // tf),
        in_specs=[
            pl.BlockSpec((tm, d), lambda i, j: (i, 0)),
            pl.BlockSpec((None, 1, d), lambda i, j: (layer, 0, 0)),
            pl.BlockSpec((None, d, tf), lambda i, j: (layer, 0, j)),
            pl.BlockSpec((None, tf, d), lambda i, j: (layer, j, 0)),
        ],
        out_specs=pl.BlockSpec((tm, d), lambda i, j: (i, 0)),
        scratch_shapes=[pltpu.VMEM((tm, d), BF16)],
        compiler_params=_params("parallel", "arbitrary", vmem=MLP_VMEM_LIMIT),
        name="mlp",
    )(x, g, wu, wd)


def _rope_tables(seq):
    inv = 1.0 / (ROPE_THETA ** (jnp.arange(0, ROPE_DIM, 2, dtype=F32) / ROPE_DIM))
    ang = jnp.arange(seq, dtype=F32)[:, None] * inv[None, :]
    cos, sin = jnp.cos(ang), jnp.sin(ang)
    reps = LANES // (ROPE_DIM // 2)
    sign = jnp.concatenate([-jnp.ones((LANES // 2,), F32), jnp.ones((LANES // 2,), F32)])
    return jnp.tile(cos, (1, reps)), jnp.tile(sin, (1, reps)) * sign[None, :]


def _pair_lanes(g):
    return jnp.repeat(g.reshape(2, ROPE_DIM // 2), LANES // ROPE_DIM, axis=0).reshape(1, LANES)


def _prepare_even(norm_mix0, w_in0, conv_w, q_a_norm, w_qb, kv_a_norm, w_kvb, q_norm, k_norm, w_o0):
    half = ROPE_DIM // 2
    base = IN0_DIM - ROPE_DIM
    w16 = w_in0.astype(BF16)
    x1, x2 = w16[:, base:base + half], w16[:, base + half:]
    pad = jnp.zeros((D_MODEL, IN0_PAD - base - LANES), BF16)
    w_in = jnp.concatenate([w16[:, :base], x1, x1, x2, x2, pad], axis=1)
    wq3 = w_qb.astype(BF16).reshape(Q_RANK, MLA_HEADS, QK_DIM)
    wq_rope = wq3[:, :, NOPE_DIM:].reshape(Q_RANK, MLA_HEADS // 2, 2, 2, half)
    wq_rope = wq_rope.transpose(0, 1, 3, 2, 4).reshape(Q_RANK, MLA_HEADS * ROPE_DIM)
    wqb = jnp.concatenate([wq3[:, :, :NOPE_DIM].reshape(Q_RANK, MLA_HEADS * NOPE_DIM), wq_rope], axis=1)
    scale = QK_DIM ** -0.5 * LOG2_E
    return dict(
        g0=norm_mix0.reshape(1, D_MODEL), w_in=w_in, cw=conv_w,
        gqa=q_a_norm.reshape(1, Q_RANK), gkva=kv_a_norm.reshape(1, KV_RANK),
        wqb=wqb, wkvb=w_kvb.astype(BF16),
        gqn=(q_norm[:NOPE_DIM] * scale).reshape(1, NOPE_DIM),
        gqr=_pair_lanes(q_norm[NOPE_DIM:] * scale),
        gkn=k_norm[:NOPE_DIM].reshape(1, NOPE_DIM),
        gkr=_pair_lanes(k_norm[NOPE_DIM:]),
        wo=jnp.concatenate([w_o0[CONV_DIM:], w_o0[:CONV_DIM]], axis=0).astype(BF16),
    )


def _tile(n, pref):
    return pref if n % pref == 0 else n


def _trunk(x3, even, odd, mlp, cos, sin):
    batch, seq, d = x3.shape
    t = batch * seq
    x = x3.reshape(t, d)
    tm_mlp = _tile(seq, TOKENS_WIDE)
    tm = _tile(seq, TOKENS_NARROW)
    tq = seq if seq <= ATTN_Q_MAX else ATTN_Q_LONG
    gm, wu, wd = mlp
    for i in range(gm.shape[0]):
        j = i // 2
        if i % 2 == 0:
            p = even[j]
            u = _inproj(x, p["g0"], p["w_in"], tm_mlp, INPROJ_COLS)
            qn, qr, k, v = _prep(u, cos, sin, p["gqa"], p["gkva"], p["wqb"], p["wkvb"],
                                 p["gqn"], p["gqr"], p["gkn"], p["gkr"], seq, tm)
            cast = ()
            if wu.dtype != BF16 and _flash_cast_rows((wu, wd), batch * MLA_HEADS * (seq // tq)):
                cast = (wu, wd)
            att, *casted = _flash(qn, qr, k, v, batch, seq, tq, ATTN_Q_SUB, _tile(seq, ATTN_KEYS), cast)
            if casted:
                wu, wd = casted
            x = _outproj(x, u, p["cw"], att, p["wo"], seq, tm)
        else:
            g1, wp, ps = odd[j]
            x = _pool(x, g1, wp, ps, seq, tm_mlp)
        if wu.dtype != BF16:
            wu, wd = wu.astype(BF16), wd.astype(BF16)
        x = _mlp(x, gm, wu, wd, i, tm_mlp, MLP_HIDDEN)
    return x.reshape(batch, seq, d), (gm, wu, wd)


def kernel(x_prompt, x_sample, norm_mix0, w_in0, conv_w, q_a_norm, w_qb, kv_a_norm, w_kvb, q_norm, k_norm, w_o0,
           norm_mix1, w_pool, pool_scale, norm_mlp, w_up, w_down):
    depth = norm_mlp.shape[0]
    even = [_prepare_even(norm_mix0[j], w_in0[j], conv_w[j], q_a_norm[j], w_qb[j], kv_a_norm[j], w_kvb[j],
                          q_norm[j], k_norm[j], w_o0[j]) for j in range(norm_mix0.shape[0])]
    odd = [(norm_mix1[j].reshape(1, D_MODEL), w_pool[j].astype(BF16), pool_scale[j].reshape(1, D_MODEL))
           for j in range(norm_mix1.shape[0])]
    mlp = (norm_mlp.reshape(depth, 1, D_MODEL), w_up, w_down)
    outs = []
    for x3 in (x_prompt, x_sample):
        cos, sin = _rope_tables(x3.shape[1])
        y, mlp = _trunk(x3, even, odd, mlp, cos, sin)
        outs.append(y)
    return tuple(outs)
```
